```python
import math
import jax, jax.numpy as jnp
from jax import lax
import numpy as np

D_MODEL = 4096
BATCH = 4
SEQ = 4096
DEPTH = 2

D_CONV = D_MODEL // 4
CONV_W = 3
N_HEADS = 16
HEAD_DIM = 128
D_ATT = N_HEADS * HEAD_DIM
D_LAT = 256
IDX_HEADS = 32
IDX_DIM = 64
TOPK_MAX = 256
Q_BLOCK = 128
POOL_WINDOWS = (2, 4, 8, 16)
N_POOL_GROUPS = 4
D_POOL = D_MODEL // 4
POOL_GROUP = D_POOL // N_POOL_GROUPS
N_BRANCH = 3
D_FF = ((8 * D_MODEL // 3 + 255) // 256) * 256
ALPHA = (2.0 * DEPTH) ** 0.25
BETA = (8.0 * DEPTH) ** -0.25
LN_EPS = 1e-5
RMS_EPS = 1e-6
IN_SPLITS = (D_CONV, D_CONV, D_CONV,
             D_ATT, D_LAT,
             IDX_HEADS * IDX_DIM, IDX_DIM, IDX_HEADS,
             D_POOL,
             N_BRANCH * D_MODEL)
N_IN = sum(IN_SPLITS)

kernel_name = "hybrid_conv_dsa_pool_deepnorm"


def layer_norm(x, g, b):
    xf = x.astype(jnp.float32)
    mu = jnp.mean(xf, axis=-1, keepdims=True)
    var = jnp.mean(jnp.square(xf - mu), axis=-1, keepdims=True)
    return ((xf - mu) * lax.rsqrt(var + LN_EPS) * g + b).astype(x.dtype)


def rms_norm(x, g):
    xf = x.astype(jnp.float32)
    return (xf * lax.rsqrt(jnp.mean(xf * xf, axis=-1, keepdims=True) + RMS_EPS) * g).astype(x.dtype)


def causal_dwconv(u, w):
    k = w.shape[0]
    s = u.shape[1]
    up = jnp.pad(u, ((0, 0), (k - 1, 0), (0, 0)))
    out = up[:, 0:s] * w[0]
    for j in range(1, k):
        out = out + up[:, j:j + s] * w[j]
    return out


def short_conv_mixer(b_gate, c_gate, v, conv_w):
    return b_gate * causal_dwconv(c_gate * v, conv_w)


def pool_mixer(u, pool_w, pool_scale):
    bsz, s, _ = u.shape
    uf = u.astype(jnp.float32)
    c0 = jnp.concatenate([jnp.zeros((bsz, 1, D_POOL), jnp.float32), jnp.cumsum(uf, axis=1)], axis=1)
    t1 = jnp.arange(1, s + 1, dtype=jnp.float32)[:, None]
    outs = []
    for g, w in enumerate(POOL_WINDOWS):
        sl = slice(g * POOL_GROUP, (g + 1) * POOL_GROUP)
        cg = c0[:, :, sl]
        prev = jnp.concatenate([jnp.zeros((bsz, w - 1, POOL_GROUP), jnp.float32), cg[:, :s - w + 1]], axis=1)
        mean = (cg[:, 1:] - prev) / jnp.minimum(t1, float(w))
        outs.append(mean - uf[:, :, sl])
    d = jnp.stack(outs, axis=2).astype(u.dtype)
    y = jnp.einsum('bsgc,gcd->bsgd', d, pool_w).reshape(bsz, s, D_POOL)
    return y * pool_scale


def dsa_attention(q, c_kv, q_idx, k_idx, w_idx, w_uk, w_uv):
    bsz, s = q.shape[0], q.shape[1]
    k_sel = min(TOPK_MAX, s // 4)
    nb = s // Q_BLOCK
    key_pos = jnp.arange(s)
    k_idx_f = k_idx.astype(jnp.float32)

    def to_blocks(a):
        return a.reshape((bsz, nb, Q_BLOCK) + a.shape[2:]).swapaxes(0, 1)

    def block(args):
        blk, qb, qib, wb = args
        tpos = blk * Q_BLOCK + jnp.arange(Q_BLOCK)
        causal = key_pos[None, :] <= tpos[:, None]
        logits = jnp.einsum('bthd,bsd->bths', qib.astype(jnp.float32), k_idx_f) * (IDX_DIM ** -0.5)
        score = jnp.einsum('bth,bths->bts', wb.astype(jnp.float32), jax.nn.relu(logits)) * (IDX_HEADS ** -0.5)
        score = jnp.where(causal[None], score, -jnp.inf)
        _, sel = lax.top_k(score, k_sel)
        ckv_sel = jax.vmap(lambda c, i: c[i])(c_kv, sel)
        valid = sel <= tpos[None, :, None]
        q_lat = jnp.einsum('bthd,hdc->bthc', qb, w_uk)
        sc = jnp.einsum('bthc,btkc->bthk', q_lat, ckv_sel).astype(jnp.float32) * (HEAD_DIM ** -0.5)
        sc = jnp.where(valid[:, :, None, :], sc, -jnp.inf)
        p = jax.nn.softmax(sc, axis=-1).astype(qb.dtype)
        o_lat = jnp.einsum('bthk,btkc->bthc', p, ckv_sel)
        return jnp.einsum('bthc,hcd->bthd', o_lat, w_uv)

    out = lax.map(block, (jnp.arange(nb), to_blocks(q), to_blocks(q_idx), to_blocks(w_idx)))
    return out.swapaxes(0, 1).reshape(bsz, s, D_ATT)


def hybrid_mixer(x, w_in, b_gate, conv_a, kv_norm, w_uk, w_uv, pool_w, pool_scale,
                 w_br_a, w_br_b, w_br_c, w_o):
    bsz, s, _ = x.shape
    z = x @ w_in
    splits = np.cumsum(IN_SPLITS)[:-1].tolist()
    bg, cg, v, q, ckv, qi, ki, wi, u_pool, gate_logits = jnp.split(z, splits, axis=-1)
    y_a = short_conv_mixer(bg, cg, v, conv_a)
    y_b = dsa_attention(q.reshape(bsz, s, N_HEADS, HEAD_DIM), rms_norm(ckv, kv_norm),
                        qi.reshape(bsz, s, IDX_HEADS, IDX_DIM), ki, wi, w_uk, w_uv)
    y_c = pool_mixer(u_pool, pool_w, pool_scale)
    g = jax.nn.sigmoid(gate_logits.reshape(bsz, s, N_BRANCH, D_MODEL) + b_gate)
    merged = (g[:, :, 0] * (y_a @ w_br_a)
              + g[:, :, 1] * (y_b @ w_br_b)
              + g[:, :, 2] * (y_c @ w_br_c))
    return merged @ w_o


def conv_ffn(x, w_up, w_conv, w_down):
    gt, up = jnp.split(x @ w_up, 2, axis=-1)
    h = jax.nn.silu(causal_dwconv(gt, w_conv)) * up
    return h @ w_down


def setup_inputs(seed: int = 0) -> dict:
    key = jax.random.key(seed)
    ks = jax.random.split(key, 24)

    def nrm(k, shape, scale):
        return jax.random.normal(k, shape, jnp.float32) * scale

    L = DEPTH
    return {
        "x": nrm(ks[0], (BATCH, SEQ, D_MODEL), 1.0),
        "w_in": nrm(ks[1], (L, D_MODEL, N_IN), D_MODEL ** -0.5),
        "b_gate": nrm(ks[2], (L, N_BRANCH, D_MODEL), 0.02),
        "conv_a": nrm(ks[3], (L, CONV_W, D_CONV), CONV_W ** -0.5),
        "kv_norm": 1.0 + nrm(ks[4], (L, D_LAT), 0.02),
        "w_uk": nrm(ks[5], (L, N_HEADS, HEAD_DIM, D_LAT), HEAD_DIM ** -0.5),
        "w_uv": nrm(ks[6], (L, N_HEADS, D_LAT, HEAD_DIM), D_LAT ** -0.5),
        "pool_w": nrm(ks[7], (L, N_POOL_GROUPS, POOL_GROUP, POOL_GROUP), POOL_GROUP ** -0.5),
        "pool_scale": 1.0 + nrm(ks[8], (L, D_POOL), 0.02),
        "w_br_a": nrm(ks[9], (L, D_CONV, D_MODEL), BETA * D_CONV ** -0.5),
        "w_br_b": nrm(ks[10], (L, D_ATT, D_MODEL), BETA * D_ATT ** -0.5),
        "w_br_c": nrm(ks[11], (L, D_POOL, D_MODEL), BETA * D_POOL ** -0.5),
        "w_o": nrm(ks[12], (L, D_MODEL, D_MODEL), BETA * D_MODEL ** -0.5),
        "ln1_g": 1.0 + nrm(ks[13], (L, D_MODEL), 0.02),
        "ln1_b": nrm(ks[14], (L, D_MODEL), 0.02),
        "w_up": nrm(ks[15], (L, D_MODEL, 2 * D_FF), D_MODEL ** -0.5),
        "conv_ffn_w": nrm(ks[16], (L, CONV_W, D_FF), CONV_W ** -0.5),
        "w_down": nrm(ks[17], (L, D_FF, D_MODEL), BETA * D_FF ** -0.5),
        "ln2_g": 1.0 + nrm(ks[18], (L, D_MODEL), 0.02),
        "ln2_b": nrm(ks[19], (L, D_MODEL), 0.02),
    }


def reference(x, w_in, b_gate, conv_a, kv_norm, w_uk, w_uv, pool_w, pool_scale,
              w_br_a, w_br_b, w_br_c, w_o, ln1_g, ln1_b, w_up, conv_ffn_w, w_down,
              ln2_g, ln2_b):
    for l in range(DEPTH):
        mix = hybrid_mixer(x, w_in[l], b_gate[l], conv_a[l], kv_norm[l], w_uk[l], w_uv[l],
                           pool_w[l], pool_scale[l], w_br_a[l], w_br_b[l], w_br_c[l], w_o[l])
        x = layer_norm(ALPHA * x + mix, ln1_g[l], ln1_b[l])
        ff = conv_ffn(x, w_up[l], conv_ffn_w[l], w_down[l])
        x = layer_norm(ALPHA * x + ff, ln2_g[l], ln2_b[l])
    return x
```

```python
import functools

import jax
import jax.numpy as jnp
from jax import lax
from jax.experimental import pallas as pl
from jax.experimental.pallas import tpu as pltpu

IDX_DIM = 64
TOPK_MAX = 256
POOL_WINDOWS = (2, 4, 8, 16)
CONV_W = 3
LN_EPS = 1e-5
RMS_EPS = 1e-6

LANES = 128
SUBLANES = 8
BF16_ROWS = 16
VMEM_BYTES_V7X = 64 * 1024 * 1024
VMEM_LIMIT = VMEM_BYTES_V7X - 8 * 1024 * 1024

HALO = BF16_ROWS
NEG_BIG = -1e30
F32 = jnp.float32
BF16 = jnp.bfloat16


def _params(n_axes):
    return pltpu.CompilerParams(dimension_semantics=("arbitrary",) * n_axes,
                                vmem_limit_bytes=VMEM_LIMIT)


def _mm_kernel(x_ref, w_ref, o_ref):
    o_ref[...] = jnp.dot(x_ref[...], w_ref[...], preferred_element_type=F32).astype(o_ref.dtype)


def _mm_res_kernel(x_ref, w_ref, r_ref, o_ref, *, alpha):
    acc = jnp.dot(x_ref[...], w_ref[...], preferred_element_type=F32)
    o_ref[...] = (alpha * r_ref[...] + acc).astype(o_ref.dtype)


def _matmul(x, w, out_dtype, tm, tn, res=None, alpha=None, name="mm"):
    m, k = x.shape
    n = w.shape[1]
    tm, tn = min(tm, m), min(tn, n)
    assert m % tm == 0 and n % tn == 0, (m, n, tm, tn)
    in_specs = [pl.BlockSpec((tm, k), lambda i, j: (i, 0)),
                pl.BlockSpec((k, tn), lambda i, j: (0, j))]
    args = [x, w]
    kern = _mm_kernel
    if res is not None:
        in_specs.append(pl.BlockSpec((tm, tn), lambda i, j: (i, j)))
        args.append(res)
        kern = functools.partial(_mm_res_kernel, alpha=alpha)
    return pl.pallas_call(
        kern, grid=(m // tm, n // tn), in_specs=in_specs,
        out_specs=pl.BlockSpec((tm, tn), lambda i, j: (i, j)),
        out_shape=jax.ShapeDtypeStruct((m, n), out_dtype),
        compiler_params=_params(2), name=name)(*args)


def _conv_a_kernel(bg_ref, cg_ref, v_ref, cgp_ref, vp_ref, w_ref, o_ref, buf_ref, *, rows):
    i = pl.program_id(1)
    u = cg_ref[...].astype(F32) * v_ref[...].astype(F32)
    up = cgp_ref[...].astype(F32) * vp_ref[...].astype(F32)
    buf_ref[0:HALO, :] = jnp.where(i > 0, up, 0.0)
    buf_ref[HALO:HALO + rows, :] = u
    s1 = buf_ref[HALO - 1:HALO - 1 + rows, :]
    s2 = buf_ref[HALO - 2:HALO - 2 + rows, :]
    w = w_ref[...]
    conv = w[0:1, :] * s2 + w[1:2, :] * s1 + w[2:3, :] * u
    o_ref[...] = (bg_ref[...].astype(F32) * conv).astype(o_ref.dtype)


def _conv_a(zh, conv_w, bsz, seq, col0, width, rows):
    t = zh.shape[0]
    nb = seq // rows
    hb = rows // HALO

    def cur(c):
        return pl.BlockSpec((rows, width), lambda b, i: (b * nb + i, col0 + c))

    def prev(c):
        return pl.BlockSpec((HALO, width), lambda b, i: (jnp.maximum((b * nb + i) * hb - 1, 0), col0 + c))

    return pl.pallas_call(
        functools.partial(_conv_a_kernel, rows=rows), grid=(bsz, nb),
        in_specs=[cur(0), cur(1), cur(2), prev(1), prev(2),
                  pl.BlockSpec((CONV_W, width), lambda b, i: (0, 0))],
        out_specs=pl.BlockSpec((rows, width), lambda b, i: (b * nb + i, 0)),
        out_shape=jax.ShapeDtypeStruct((t, width), BF16),
        scratch_shapes=[pltpu.VMEM((HALO + rows, width), F32)],
        compiler_params=_params(2), name="conv_a")(zh, zh, zh, zh, zh, conv_w)


def _pool_kernel(u_ref, up_ref, pw_ref, ps_ref, o_ref, buf_ref, *, rows, group):
    i = pl.program_id(1)
    ext = HALO + rows
    pos = i * rows + lax.broadcasted_iota(jnp.int32, (rows, 1), 0)
    buf_ref[0:HALO, :] = jnp.zeros((HALO, group), F32)
    for g, win in enumerate(POOL_WINDOWS):
        sl = slice(g * group, (g + 1) * group)
        u = u_ref[:, sl].astype(F32)
        buf_ref[HALO:2 * HALO, :] = jnp.where(i > 0, up_ref[:, sl].astype(F32), 0.0)
        buf_ref[2 * HALO:2 * HALO + rows, :] = u
        k = 1
        while k < win:
            s = buf_ref[HALO:HALO + ext, :] + buf_ref[HALO - k:HALO - k + ext, :]
            buf_ref[HALO:HALO + ext, :] = s
            k *= 2
        wsum = buf_ref[2 * HALO:2 * HALO + rows, :]
        cnt = jnp.minimum(pos + 1, win).astype(F32)
        d = wsum / cnt - u
        y = jnp.dot(d.astype(BF16), pw_ref[g], preferred_element_type=F32)
        o_ref[:, sl] = (y * ps_ref[:, sl]).astype(o_ref.dtype)


def _pool(zh, pool_w, pool_scale, bsz, seq, col0, rows):
    t = zh.shape[0]
    ngroups, group, _ = pool_w.shape
    width = ngroups * group
    nb = seq // rows
    hb = rows // HALO
    return pl.pallas_call(
        functools.partial(_pool_kernel, rows=rows, group=group), grid=(bsz, nb),
        in_specs=[pl.BlockSpec((rows, width), lambda b, i: (b * nb + i, col0)),
                  pl.BlockSpec((HALO, width), lambda b, i: (jnp.maximum((b * nb + i) * hb - 1, 0), col0)),
                  pl.BlockSpec((ngroups, group, group), lambda b, i: (0, 0, 0)),
                  pl.BlockSpec((1, width), lambda b, i: (0, 0))],
        out_specs=pl.BlockSpec((rows, width), lambda b, i: (b * nb + i, 0)),
        out_shape=jax.ShapeDtypeStruct((t, width), BF16),
        scratch_shapes=[pltpu.VMEM((2 * HALO + rows, group), F32)],
        compiler_params=_params(2), name="pool")(zh, zh, pool_w, pool_scale)


def _ckv_kernel(z_ref, g_ref, o_ref, *, d_lat):
    c = z_ref[:, 0:d_lat]
    ms = jnp.mean(c * c, axis=-1, keepdims=True)
    o_ref[...] = (c * lax.rsqrt(ms + RMS_EPS) * g_ref[...]).astype(o_ref.dtype)


def _ckv(zs, kv_norm, d_lat, rows):
    t, ns = zs.shape
    return pl.pallas_call(
        functools.partial(_ckv_kernel, d_lat=d_lat), grid=(t // rows,),
        in_specs=[pl.BlockSpec((rows, ns), lambda i: (i, 0)),
                  pl.BlockSpec((1, d_lat), lambda i: (0, 0))],
        out_specs=pl.BlockSpec((rows, d_lat), lambda i: (i, 0)),
        out_shape=jax.ShapeDtypeStruct((t, d_lat), BF16),
        compiler_params=_params(1), name="ckv_norm")(zs, kv_norm)


def _qlat_kernel(q_ref, w_ref, o_ref):
    o_ref[...] = jnp.dot(q_ref[...], w_ref[0], preferred_element_type=F32).astype(o_ref.dtype)


def _qlat(zh, w_uk, col0, rows):
    t = zh.shape[0]
    nh, hd, dl = w_uk.shape
    return pl.pallas_call(
        _qlat_kernel, grid=(t // rows, nh),
        in_specs=[pl.BlockSpec((rows, hd), lambda i, h: (i, col0 + h)),
                  pl.BlockSpec((1, hd, dl), lambda i, h: (h, 0, 0))],
        out_specs=pl.BlockSpec((rows, dl), lambda i, h: (i, h)),
        out_shape=jax.ShapeDtypeStruct((t, nh * dl), BF16),
        compiler_params=_params(2), name="q_lat")(zh, w_uk)


def _count_ge(key_ref, nkc, ts, tq, pred):
    def body(kc, part):
        m = pred(key_ref[:, pl.ds(pl.multiple_of(kc * ts, ts), ts)], kc).astype(F32)
        for c in range(ts // LANES):
            part = part + m[:, c * LANES:(c + 1) * LANES]
        return part

    part = lax.fori_loop(0, nkc, body, jnp.zeros((tq, LANES), F32))
    return jnp.sum(part, axis=1, keepdims=True)


def _bisect_largest(nbits, target, count_fn, tq):
    sign = jnp.int32(-2 ** 31) if nbits == 32 else jnp.int32(0)

    def body(b, v):
        bit = lax.shift_left(jnp.int32(1), jnp.int32(nbits - 1) - b)
        cand = v | bit
        cnt = count_fn(cand ^ sign)
        return jnp.where(cnt >= target, cand, v)

    v = lax.fori_loop(0, nbits, body, jnp.zeros((tq, 1), jnp.int32))
    return v ^ sign


def _attn_kernel(qi_ref, zq_ref, zk_ref, ckv_ref, ql_ref, wuv_ref, o_ref,
                 key_ref, m_ref, l_ref, acc_ref, *, tq, ts, seq, n_idx, n_heads, d_lat, head_dim,
                 k_sel, idx_scale, sm_scale):
    i = pl.program_id(1)
    q0 = i * tq
    nkc = (q0 + tq + ts - 1) // ts
    nbits_idx = max(1, (seq - 1).bit_length())
    row = q0 + lax.broadcasted_iota(jnp.int32, (tq, 1), 0)
    int_min = jnp.int32(-2 ** 31)

    wq = zq_ref[:, IDX_DIM:IDX_DIM + n_idx] * idx_scale

    def score_body(kc, carry):
        s0 = pl.multiple_of(kc * ts, ts)
        ki = zk_ref[pl.ds(s0, ts), 0:IDX_DIM].astype(BF16)
        acc = jnp.zeros((tq, ts), F32)
        for h in range(n_idx):
            lg = lax.dot_general(qi_ref[:, h * IDX_DIM:(h + 1) * IDX_DIM], ki,
                                 (((1,), (1,)), ((), ())), preferred_element_type=F32)
            acc = acc + wq[:, h:h + 1] * jnp.maximum(lg, 0.0)
        bits = lax.bitcast_convert_type(acc, jnp.int32)
        key = bits ^ (lax.shift_right_arithmetic(bits, 31) & jnp.int32(0x7FFFFFFF))
        col = s0 + lax.broadcasted_iota(jnp.int32, (tq, ts), 1)
        key_ref[:, pl.ds(s0, ts)] = jnp.where(col <= row, key, int_min)
        return carry

    lax.fori_loop(0, nkc, score_body, 0)

    kf = jnp.float32(k_sel)
    thr = _bisect_largest(32, kf, lambda v: _count_ge(key_ref, nkc, ts, tq, lambda x, kc: x >= v), tq)
    n_gt = _count_ge(key_ref, nkc, ts, tq, lambda x, kc: x > thr)
    need = kf - n_gt

    def rev_idx(kc):
        return (seq - 1) - (kc * ts + lax.broadcasted_iota(jnp.int32, (tq, ts), 1))

    cut = _bisect_largest(
        nbits_idx, need,
        lambda v: _count_ge(key_ref, nkc, ts, tq, lambda x, kc: (x == thr) & (rev_idx(kc) >= v)), tq)

    m_ref[...] = jnp.full(m_ref.shape, NEG_BIG, F32)
    l_ref[...] = jnp.zeros(l_ref.shape, F32)
    acc_ref[...] = jnp.zeros(acc_ref.shape, F32)
    qs = jnp.concatenate([ql_ref[:, h * d_lat:(h + 1) * d_lat] for h in range(n_heads)], axis=0)

    def attn_body(kc, carry):
        s0 = pl.multiple_of(kc * ts, ts)
        x = key_ref[:, pl.ds(s0, ts)]
        col = s0 + lax.broadcasted_iota(jnp.int32, (tq, ts), 1)
        sel = ((x > thr) | ((x == thr) & (rev_idx(kc) >= cut))) & (col <= row)
        bias = jnp.where(sel, 0.0, NEG_BIG)
        ckv = ckv_ref[pl.ds(s0, ts), :]
        sc = lax.dot_general(qs, ckv, (((1,), (1,)), ((), ())), preferred_element_type=F32) * sm_scale
        sc = sc.reshape(n_heads, tq, ts) + bias[None]
        m_old = m_ref[...]
        m_new = jnp.maximum(m_old, jnp.max(sc, axis=-1, keepdims=True))
        alpha = jnp.exp(m_old - m_new)
        p = jnp.exp(sc - m_new)
        l_ref[...] = alpha * l_ref[...] + jnp.sum(p, axis=-1, keepdims=True)
        pv = jnp.dot(p.reshape(n_heads * tq, ts).astype(BF16), ckv, preferred_element_type=F32)
        acc_ref[...] = alpha * acc_ref[...] + pv.reshape(n_heads, tq, d_lat)
        m_ref[...] = m_new
        return carry

    lax.fori_loop(0, nkc, attn_body, 0)

    o_lat = (acc_ref[...] / l_ref[...]).astype(BF16)
    for h in range(n_heads):
        o_ref[:, h * head_dim:(h + 1) * head_dim] = jnp.dot(
            o_lat[h], wuv_ref[h], preferred_element_type=F32).astype(o_ref.dtype)


def _attention(zh, zs, ckv, qlat, w_uv, bsz, seq, qi_col, n_idx, kw_col, tq, ts):
    t = zh.shape[0]
    nh, dl, hd = w_uv.shape
    nq = seq // tq
    k_sel = min(TOPK_MAX, seq // 4)
    assert ts >= k_sel and seq % ts == 0 and seq % tq == 0 and ts % tq == 0
    kern = functools.partial(
        _attn_kernel, tq=tq, ts=ts, seq=seq, n_idx=n_idx, n_heads=nh, d_lat=dl, head_dim=hd, k_sel=k_sel,
        idx_scale=float(IDX_DIM ** -0.5 * n_idx ** -0.5), sm_scale=float(hd ** -0.5))
    return pl.pallas_call(
        kern, grid=(bsz, nq),
        in_specs=[pl.BlockSpec((tq, n_idx * IDX_DIM), lambda b, i: (b * nq + i, qi_col)),
                  pl.BlockSpec((tq, LANES), lambda b, i: (b * nq + i, kw_col)),
                  pl.BlockSpec((seq, LANES), lambda b, i: (b, kw_col)),
                  pl.BlockSpec((seq, dl), lambda b, i: (b, 0)),
                  pl.BlockSpec((tq, nh * dl), lambda b, i: (b * nq + i, 0)),
                  pl.BlockSpec((nh, dl, hd), lambda b, i: (0, 0, 0))],
        out_specs=pl.BlockSpec((tq, nh * hd), lambda b, i: (b * nq + i, 0)),
        out_shape=jax.ShapeDtypeStruct((t, nh * hd), BF16),
        scratch_shapes=[pltpu.VMEM((tq, seq), jnp.int32),
                        pltpu.VMEM((nh, tq, 1), F32), pltpu.VMEM((nh, tq, 1), F32),
                        pltpu.VMEM((nh, tq, dl), F32)],
        compiler_params=_params(2), name="dsa_attention")(zh, zs, zs, ckv, qlat, w_uv)


def _merge_kernel(ya_ref, yb_ref, yc_ref, wa_ref, wb_ref, wc_ref, g0_ref, g1_ref, g2_ref, b_ref, o_ref):
    b = b_ref[...]
    out = None
    for k, (y_ref, w_ref, g_ref) in enumerate(((ya_ref, wa_ref, g0_ref), (yb_ref, wb_ref, g1_ref),
                                                (yc_ref, wc_ref, g2_ref))):
        gate = jax.nn.sigmoid(g_ref[...].astype(F32) + b[k:k + 1, :])
        term = gate * jnp.dot(y_ref[...], w_ref[...], preferred_element_type=F32)
        out = term if out is None else out + term
    o_ref[...] = out.astype(o_ref.dtype)


def _merge(ya, yb, yc, wa, wb, wc, zg, b_gate, tm, tn):
    t = ya.shape[0]
    d = wa.shape[1]
    nj = d // tn

    def yspec(y):
        return pl.BlockSpec((tm, y.shape[1]), lambda i, j: (i, 0))

    def wspec(w):
        return pl.BlockSpec((w.shape[0], tn), lambda i, j: (0, j))

    def gspec(k):
        return pl.BlockSpec((tm, tn), lambda i, j: (i, k * nj + j))

    return pl.pallas_call(
        _merge_kernel, grid=(t // tm, nj),
        in_specs=[yspec(ya), yspec(yb), yspec(yc), wspec(wa), wspec(wb), wspec(wc),
                  gspec(0), gspec(1), gspec(2), pl.BlockSpec((b_gate.shape[0], tn), lambda i, j: (0, j))],
        out_specs=pl.BlockSpec((tm, tn), lambda i, j: (i, j)),
        out_shape=jax.ShapeDtypeStruct((t, d), BF16),
        compiler_params=_params(2), name="merge")(ya, yb, yc, wa, wb, wc, zg, zg, zg, b_gate)


def _ln_kernel(y_ref, g_ref, b_ref, o_ref, ob_ref):
    y = y_ref[...]
    mu = jnp.mean(y, axis=-1, keepdims=True)
    yc = y - mu
    var = jnp.mean(yc * yc, axis=-1, keepdims=True)
    out = yc * lax.rsqrt(var + LN_EPS) * g_ref[...] + b_ref[...]
    o_ref[...] = out
    ob_ref[...] = out.astype(BF16)


def _layer_norm(y, g, b, rows):
    t, d = y.shape
    row_spec = pl.BlockSpec((rows, d), lambda i: (i, 0))
    vec_spec = pl.BlockSpec((1, d), lambda i: (0, 0))
    return pl.pallas_call(
        _ln_kernel, grid=(t // rows,), in_specs=[row_spec, vec_spec, vec_spec],
        out_specs=[row_spec, row_spec],
        out_shape=[jax.ShapeDtypeStruct((t, d), F32), jax.ShapeDtypeStruct((t, d), BF16)],
        compiler_params=_params(1), name="layer_norm")(y, g, b)


def _ffn_up_kernel(x_ref, wg_ref, wu_ref, cw_ref, o_ref, buf_ref, carry_ref, *, tm, tiles_per_seq):
    i = pl.program_id(0)
    j = pl.program_id(1)
    gt = jnp.dot(x_ref[...], wg_ref[...], preferred_element_type=F32)
    up = jnp.dot(x_ref[...], wu_ref[...], preferred_element_type=F32)
    seq_start = i % tiles_per_seq == 0

    @pl.when(seq_start)
    def _():
        buf_ref[0:SUBLANES, :] = jnp.zeros((SUBLANES, gt.shape[1]), F32)

    @pl.when(jnp.logical_not(seq_start))
    def _():
        buf_ref[0:SUBLANES, :] = carry_ref[j]

    buf_ref[SUBLANES:SUBLANES + tm, :] = gt
    carry_ref[j] = gt[tm - SUBLANES:tm, :]
    s1 = buf_ref[SUBLANES - 1:SUBLANES - 1 + tm, :]
    s2 = buf_ref[SUBLANES - 2:SUBLANES - 2 + tm, :]
    w = cw_ref[...]
    conv = w[0:1, :] * s2 + w[1:2, :] * s1 + w[2:3, :] * gt
    o_ref[...] = (jax.nn.silu(conv) * up).astype(o_ref.dtype)


def _ffn_up(xb, wg, wu, conv_w, seq, tm, tn):
    t, d = xb.shape
    dff = wg.shape[1]
    assert seq % tm == 0 and dff % tn == 0
    nj = dff // tn
    return pl.pallas_call(
        functools.partial(_ffn_up_kernel, tm=tm, tiles_per_seq=seq // tm), grid=(t // tm, nj),
        in_specs=[pl.BlockSpec((tm, d), lambda i, j: (i, 0)),
                  pl.BlockSpec((d, tn), lambda i, j: (0, j)),
                  pl.BlockSpec((d, tn), lambda i, j: (0, j)),
                  pl.BlockSpec((CONV_W, tn), lambda i, j: (0, j))],
        out_specs=pl.BlockSpec((tm, tn), lambda i, j: (i, j)),
        out_shape=jax.ShapeDtypeStruct((t, dff), BF16),
        scratch_shapes=[pltpu.VMEM((SUBLANES + tm, tn), F32), pltpu.VMEM((nj, SUBLANES, tn), F32)],
        compiler_params=_params(2), name="ffn_up")(xb, wg, wu, conv_w)


def _tiles(seq, d_model):
    big = d_model >= 4096
    return dict(
        mm=(1024, 1024) if big else (256, 512),
        down=(512, 512) if big else (256, 512),
        ffn_up=(1024, 512) if big else (256, 512),
        merge=(1024, 512) if big else (256, 512),
        rows=512 if big else 256,
        ln_rows=256,
        tq=128, ts=512)


def _round_up(x, m):
    return (x + m - 1) // m * m


def kernel(x, w_in, b_gate, conv_a, kv_norm, w_uk, w_uv, pool_w, pool_scale, w_br_a, w_br_b, w_br_c,
           w_o, ln1_g, ln1_b, w_up, conv_ffn_w, w_down, ln2_g, ln2_b):
    bsz, seq, d = x.shape
    depth = w_in.shape[0]
    alpha = (2.0 * depth) ** 0.25
    d_conv = conv_a.shape[2]
    n_heads, head_dim, d_lat = w_uk.shape[1:]
    d_att = n_heads * head_dim
    d_pool = pool_scale.shape[1]
    d_ff = conv_ffn_w.shape[2]
    n_in = w_in.shape[2]
    n_idx = (n_in - (3 * d_conv + d_att + d_lat + IDX_DIM + d_pool + b_gate.shape[1] * d)) // (IDX_DIM + 1)
    t = bsz * seq
    tl = _tiles(seq, d)

    o_bg, o_cg, o_v = 0, d_conv, 2 * d_conv
    o_q = 3 * d_conv
    o_ckv = o_q + d_att
    o_qi = o_ckv + d_lat
    o_ki = o_qi + n_idx * IDX_DIM
    o_wi = o_ki + IDX_DIM
    o_up = o_wi + n_idx
    o_g = o_up + d_pool
    assert o_g + b_gate.shape[1] * d == n_in
    assert IDX_DIM + n_idx <= LANES

    h_qi, h_q, h_a, h_up = 0, n_idx * IDX_DIM, n_idx * IDX_DIM + d_att, n_idx * IDX_DIM + d_att + 3 * d_conv
    n_h = h_up + d_pool
    assert h_q % head_dim == 0 and h_a % d_conv == 0 and h_up % d_pool == 0 and h_qi % (n_idx * IDX_DIM) == 0
    n_s = d_lat + LANES
    d_ffp = _round_up(d_ff, tl["ffn_up"][1])

    xf = x.reshape(t, d)
    xb = xf.astype(BF16)
    for l in range(depth):
        wl = w_in[l]
        w_h = jnp.concatenate([wl[:, o_qi:o_ki], wl[:, o_q:o_ckv], wl[:, o_bg:o_q], wl[:, o_up:o_g]],
                              axis=1).astype(BF16)
        w_s = jnp.concatenate([wl[:, o_ckv:o_qi], wl[:, o_ki:o_up],
                               jnp.zeros((d, LANES - IDX_DIM - n_idx), wl.dtype)], axis=1).astype(BF16)
        w_g = wl[:, o_g:].astype(BF16)

        zh = _matmul(xb, w_h, BF16, *tl["mm"], name="proj_wide")
        zs = _matmul(xb, w_s, F32, tl["mm"][0], n_s, name="proj_small")
        zg = _matmul(xb, w_g, BF16, *tl["mm"], name="proj_gates")

        ya = _conv_a(zh, conv_a[l], bsz, seq, h_a // d_conv, d_conv, tl["rows"])
        yc = _pool(zh, pool_w[l].astype(BF16), pool_scale[l].reshape(1, d_pool), bsz, seq, h_up // d_pool,
                   tl["rows"])
        ckv = _ckv(zs, kv_norm[l].reshape(1, d_lat), d_lat, tl["rows"])
        qlat = _qlat(zh, w_uk[l].astype(BF16), h_q // head_dim, tl["mm"][0])
        yb = _attention(zh, zs, ckv, qlat, w_uv[l].astype(BF16), bsz, seq, h_qi // (n_idx * IDX_DIM), n_idx,
                        d_lat // LANES, tl["tq"], min(tl["ts"], seq))

        merged = _merge(ya, yb, yc, w_br_a[l].astype(BF16), w_br_b[l].astype(BF16), w_br_c[l].astype(BF16),
                        zg, b_gate[l], *tl["merge"])
        y1 = _matmul(merged, w_o[l].astype(BF16), F32, *tl["mm"], res=xf, alpha=alpha, name="proj_out")
        xf, xb = _layer_norm(y1, ln1_g[l].reshape(1, d), ln1_b[l].reshape(1, d), tl["ln_rows"])

        pad = d_ffp - d_ff
        wg = jnp.pad(w_up[l][:, :d_ff], ((0, 0), (0, pad))).astype(BF16)
        wu = jnp.pad(w_up[l][:, d_ff:], ((0, 0), (0, pad))).astype(BF16)
        cw = jnp.pad(conv_ffn_w[l], ((0, 0), (0, pad)))
        wd = jnp.pad(w_down[l], ((0, pad), (0, 0))).astype(BF16)
        h = _ffn_up(xb, wg, wu, cw, seq, *tl["ffn_up"])
        y2 = _matmul(h, wd, F32, *tl["down"], res=xf, alpha=alpha, name="ffn_down")
        xf, xb = _layer_norm(y2, ln2_g[l].reshape(1, d), ln2_b[l].reshape(1, d), tl["ln_rows"])
    return xf.reshape(bsz, seq, d)
```

```python
import functools

import jax
import jax.numpy as jnp
from jax import lax
from jax.experimental import pallas as pl
from jax.experimental.pallas import tpu as pltpu

IDX_DIM = 64
TOPK_MAX = 256
POOL_WINDOWS = (2, 4, 8, 16)
CONV_W = 3
LN_EPS = 1e-5
RMS_EPS = 1e-6

LANES = 128
SUBLANES = 8
BF16_ROWS = 16
VMEM_BYTES_V7X = 64 * 1024 * 1024
VMEM_LIMIT = VMEM_BYTES_V7X - 8 * 1024 * 1024

HALO = BF16_ROWS
SCORE_ROWS = 128
NEG_BIG = -1e30
LOG2E = 1.4426950408889634
F32 = jnp.float32
BF16 = jnp.bfloat16


def _params(n_axes):
    return pltpu.CompilerParams(dimension_semantics=("arbitrary",) * n_axes,
                                vmem_limit_bytes=VMEM_LIMIT)


def _mm_kernel(x_ref, w_ref, o_ref):
    o_ref[...] = jnp.dot(x_ref[...], w_ref[...], preferred_element_type=F32).astype(o_ref.dtype)


def _mm_res_kernel(x_ref, w_ref, r_ref, o_ref, *, alpha):
    acc = jnp.dot(x_ref[...], w_ref[...], preferred_element_type=F32)
    o_ref[...] = (alpha * r_ref[...] + acc).astype(o_ref.dtype)


def _matmul(x, w, out_dtype, tm, tn, res=None, alpha=None, name="mm"):
    m, k = x.shape
    n = w.shape[1]
    tm, tn = min(tm, m), min(tn, n)
    assert m % tm == 0 and n % tn == 0, (m, n, tm, tn)
    in_specs = [pl.BlockSpec((tm, k), lambda i, j: (i, 0)),
                pl.BlockSpec((k, tn), lambda i, j: (0, j))]
    args = [x, w]
    kern = _mm_kernel
    if res is not None:
        in_specs.append(pl.BlockSpec((tm, tn), lambda i, j: (i, j)))
        args.append(res)
        kern = functools.partial(_mm_res_kernel, alpha=alpha)
    return pl.pallas_call(
        kern, grid=(m // tm, n // tn), in_specs=in_specs,
        out_specs=pl.BlockSpec((tm, tn), lambda i, j: (i, j)),
        out_shape=jax.ShapeDtypeStruct((m, n), out_dtype),
        compiler_params=_params(2), name=name)(*args)


def _conv_a_kernel(bg_ref, cg_ref, v_ref, cgp_ref, vp_ref, w_ref, o_ref, buf_ref, *, rows):
    i = pl.program_id(1)
    u = cg_ref[...].astype(F32) * v_ref[...].astype(F32)
    up = cgp_ref[...].astype(F32) * vp_ref[...].astype(F32)
    buf_ref[0:HALO, :] = jnp.where(i > 0, up, 0.0)
    buf_ref[HALO:HALO + rows, :] = u
    s1 = buf_ref[HALO - 1:HALO - 1 + rows, :]
    s2 = buf_ref[HALO - 2:HALO - 2 + rows, :]
    w = w_ref[...]
    conv = w[0:1, :] * s2 + w[1:2, :] * s1 + w[2:3, :] * u
    o_ref[...] = (bg_ref[...].astype(F32) * conv).astype(o_ref.dtype)


def _conv_a(zh, conv_w, bsz, seq, col0, width, rows):
    t = zh.shape[0]
    nb = seq // rows
    hb = rows // HALO

    def cur(c):
        return pl.BlockSpec((rows, width), lambda b, i: (b * nb + i, col0 + c))

    def prev(c):
        return pl.BlockSpec((HALO, width), lambda b, i: (jnp.maximum((b * nb + i) * hb - 1, 0), col0 + c))

    return pl.pallas_call(
        functools.partial(_conv_a_kernel, rows=rows), grid=(bsz, nb),
        in_specs=[cur(0), cur(1), cur(2), prev(1), prev(2),
                  pl.BlockSpec((CONV_W, width), lambda b, i: (0, 0))],
        out_specs=pl.BlockSpec((rows, width), lambda b, i: (b * nb + i, 0)),
        out_shape=jax.ShapeDtypeStruct((t, width), BF16),
        scratch_shapes=[pltpu.VMEM((HALO + rows, width), F32)],
        compiler_params=_params(2), name="conv_a")(zh, zh, zh, zh, zh, conv_w)


def _pool_kernel(u_ref, up_ref, pw_ref, ps_ref, o_ref, buf_ref, *, rows, group):
    i = pl.program_id(1)
    ext = HALO + rows
    pos = i * rows + lax.broadcasted_iota(jnp.int32, (rows, 1), 0)
    buf_ref[0:HALO, :] = jnp.zeros((HALO, group), F32)
    for g, win in enumerate(POOL_WINDOWS):
        sl = slice(g * group, (g + 1) * group)
        u = u_ref[:, sl].astype(F32)
        buf_ref[HALO:2 * HALO, :] = jnp.where(i > 0, up_ref[:, sl].astype(F32), 0.0)
        buf_ref[2 * HALO:2 * HALO + rows, :] = u
        k = 1
        while k < win:
            s = buf_ref[HALO:HALO + ext, :] + buf_ref[HALO - k:HALO - k + ext, :]
            buf_ref[HALO:HALO + ext, :] = s
            k *= 2
        wsum = buf_ref[2 * HALO:2 * HALO + rows, :]
        cnt = jnp.minimum(pos + 1, win).astype(F32)
        d = wsum / cnt - u
        y = jnp.dot(d.astype(BF16), pw_ref[g], preferred_element_type=F32)
        o_ref[:, sl] = (y * ps_ref[:, sl]).astype(o_ref.dtype)


def _pool(zh, pool_w, pool_scale, bsz, seq, col0, rows):
    t = zh.shape[0]
    ngroups, group, _ = pool_w.shape
    width = ngroups * group
    nb = seq // rows
    hb = rows // HALO
    return pl.pallas_call(
        functools.partial(_pool_kernel, rows=rows, group=group), grid=(bsz, nb),
        in_specs=[pl.BlockSpec((rows, width), lambda b, i: (b * nb + i, col0)),
                  pl.BlockSpec((HALO, width), lambda b, i: (jnp.maximum((b * nb + i) * hb - 1, 0), col0)),
                  pl.BlockSpec((ngroups, group, group), lambda b, i: (0, 0, 0)),
                  pl.BlockSpec((1, width), lambda b, i: (0, 0))],
        out_specs=pl.BlockSpec((rows, width), lambda b, i: (b * nb + i, 0)),
        out_shape=jax.ShapeDtypeStruct((t, width), BF16),
        scratch_shapes=[pltpu.VMEM((2 * HALO + rows, group), F32)],
        compiler_params=_params(2), name="pool")(zh, zh, pool_w, pool_scale)


def _ckv_kernel(z_ref, g_ref, o_ref, *, d_lat):
    c = z_ref[:, 0:d_lat]
    ms = jnp.mean(c * c, axis=-1, keepdims=True)
    o_ref[...] = (c * lax.rsqrt(ms + RMS_EPS) * g_ref[...]).astype(o_ref.dtype)


def _ckv(zs, kv_norm, d_lat, rows):
    t, ns = zs.shape
    return pl.pallas_call(
        functools.partial(_ckv_kernel, d_lat=d_lat), grid=(t // rows,),
        in_specs=[pl.BlockSpec((rows, ns), lambda i: (i, 0)),
                  pl.BlockSpec((1, d_lat), lambda i: (0, 0))],
        out_specs=pl.BlockSpec((rows, d_lat), lambda i: (i, 0)),
        out_shape=jax.ShapeDtypeStruct((t, d_lat), BF16),
        compiler_params=_params(1), name="ckv_norm")(zs, kv_norm)


def _qlat_kernel(q_ref, w_ref, o_ref):
    o_ref[...] = jnp.dot(q_ref[...], w_ref[0], preferred_element_type=F32).astype(o_ref.dtype)


def _qlat(zh, w_uk, col0, rows):
    t = zh.shape[0]
    nh, hd, dl = w_uk.shape
    return pl.pallas_call(
        _qlat_kernel, grid=(t // rows, nh),
        in_specs=[pl.BlockSpec((rows, hd), lambda i, h: (i, col0 + h)),
                  pl.BlockSpec((1, hd, dl), lambda i, h: (h, 0, 0))],
        out_specs=pl.BlockSpec((rows, dl), lambda i, h: (i, h)),
        out_shape=jax.ShapeDtypeStruct((t, nh * dl), BF16),
        compiler_params=_params(2), name="q_lat")(zh, w_uk)


def _count_ge(key_ref, nkc, ts, tq, pred):
    def body(kc, part):
        m = pred(key_ref[:, pl.ds(pl.multiple_of(kc * ts, ts), ts)], kc).astype(F32)
        for c in range(ts // LANES):
            part = part + m[:, c * LANES:(c + 1) * LANES]
        return part

    part = lax.fori_loop(0, nkc, body, jnp.zeros((tq, LANES), F32))
    return jnp.sum(part, axis=1, keepdims=True)


def _bisect_largest(nbits, target, count_fn, tq):
    sign = jnp.int32(-2 ** 31) if nbits == 32 else jnp.int32(0)

    def body(b, v):
        bit = lax.shift_left(jnp.int32(1), jnp.int32(nbits - 1) - b)
        cand = v | bit
        cnt = count_fn(cand ^ sign)
        return jnp.where(cnt >= target, cand, v)

    v = lax.fori_loop(0, nbits, body, jnp.zeros((tq, 1), jnp.int32))
    return v ^ sign


def _attn_kernel(qi_ref, zq_ref, zk_ref, ckv_ref, ql_ref, wuv_ref, o_ref,
                 key_ref, cut_ref, bias_ref, qs_ref, m_ref, l_ref, acc_ref, *, tq, ts, seq, n_idx,
                 n_heads, d_lat, head_dim, k_sel, idx_scale, sm_scale, slab, grp):
    i = pl.program_id(1)
    q0 = i * tq
    nkc = (q0 + tq + ts - 1) // ts
    nbits_idx = max(1, (seq - 1).bit_length())
    row = q0 + lax.broadcasted_iota(jnp.int32, (tq, 1), 0)
    int_min = jnp.int32(-2 ** 31)

    wq = zq_ref[:, IDX_DIM:IDX_DIM + n_idx] * idx_scale

    def score_body(kc, carry):
        s0 = pl.multiple_of(kc * ts, ts)
        ki = zk_ref[pl.ds(s0, ts), 0:IDX_DIM].astype(BF16)
        col = s0 + lax.broadcasted_iota(jnp.int32, (SCORE_ROWS, ts), 1)
        for r0 in range(0, tq, SCORE_ROWS):
            acc = jnp.zeros((SCORE_ROWS, ts), F32)
            for h in range(n_idx):
                lg = lax.dot_general(qi_ref[r0:r0 + SCORE_ROWS, h * IDX_DIM:(h + 1) * IDX_DIM], ki,
                                     (((1,), (1,)), ((), ())), preferred_element_type=F32)
                acc = acc + wq[r0:r0 + SCORE_ROWS, h:h + 1] * jnp.maximum(lg, 0.0)
            bits = lax.bitcast_convert_type(acc, jnp.int32)
            key = bits ^ (lax.shift_right_arithmetic(bits, 31) & jnp.int32(0x7FFFFFFF))
            key_ref[r0:r0 + SCORE_ROWS, pl.ds(s0, ts)] = jnp.where(
                col <= row[r0:r0 + SCORE_ROWS], key, int_min)
        return carry

    lax.fori_loop(0, nkc, score_body, 0)

    kf = jnp.float32(k_sel)
    thr = _bisect_largest(32, kf, lambda v: _count_ge(key_ref, nkc, ts, tq, lambda x, kc: x >= v), tq)
    n_ge = _count_ge(key_ref, nkc, ts, tq, lambda x, kc: x >= thr)

    def rev_idx(kc):
        return (seq - 1) - (kc * ts + lax.broadcasted_iota(jnp.int32, (tq, ts), 1))

    cut_ref[...] = jnp.zeros((tq, LANES), jnp.int32)

    @pl.when(jnp.max(n_ge) > kf)
    def _():
        n_gt = _count_ge(key_ref, nkc, ts, tq, lambda x, kc: x > thr)
        need = kf - n_gt
        c = _bisect_largest(
            nbits_idx, need,
            lambda v: _count_ge(key_ref, nkc, ts, tq, lambda x, kc: (x == thr) & (rev_idx(kc) >= v)), tq)
        cut_ref[...] = jnp.broadcast_to(c, (tq, LANES))

    cut = cut_ref[:, 0:1]

    rows_all = n_heads * tq
    nlc = ts // LANES
    k2 = sm_scale * LOG2E
    m_ref[...] = jnp.full((rows_all, LANES), NEG_BIG, F32)
    l_ref[...] = jnp.zeros((rows_all, LANES), F32)
    acc_ref[...] = jnp.zeros((rows_all, d_lat), F32)
    for h in range(n_heads):
        qs_ref[h * tq:(h + 1) * tq, :] = ql_ref[:, h * d_lat:(h + 1) * d_lat]

    def attn_body(kc, carry):
        s0 = pl.multiple_of(kc * ts, ts)
        x = key_ref[:, pl.ds(s0, ts)]
        col = s0 + lax.broadcasted_iota(jnp.int32, (tq, ts), 1)
        sel = ((x > thr) | ((x == thr) & (rev_idx(kc) >= cut))) & (col <= row)
        bias_ref[...] = jnp.where(sel, 0.0, NEG_BIG)
        ckv = ckv_ref[pl.ds(s0, ts), :]

        def group_body(g, c2):
            g0 = pl.multiple_of(g * grp, grp)
            sc = lax.dot_general(qs_ref[pl.ds(g0, grp), :], ckv, (((1,), (1,)), ((), ())),
                                 preferred_element_type=F32)
            alphas, probs = [], []
            for s in range(grp // slab):
                r0, b0 = s * slab, (s * slab) % tq
                xs = [sc[r0:r0 + slab, c * LANES:(c + 1) * LANES] * k2
                      + bias_ref[b0:b0 + slab, c * LANES:(c + 1) * LANES] for c in range(nlc)]
                mx = functools.reduce(jnp.maximum, xs)
                m_old = m_ref[pl.ds(g0 + r0, slab), :]
                m_new = jnp.maximum(m_old, jnp.max(mx, axis=1, keepdims=True))
                alpha = jnp.exp2(m_old - m_new)
                ps = [jnp.exp2(xc - m_new) for xc in xs]
                psum = functools.reduce(lambda a, b: a + b, ps)
                l_ref[pl.ds(g0 + r0, slab), :] = (alpha * l_ref[pl.ds(g0 + r0, slab), :]
                                                  + jnp.sum(psum, axis=1, keepdims=True))
                m_ref[pl.ds(g0 + r0, slab), :] = m_new
                alphas.append(jnp.concatenate([alpha] * (d_lat // LANES), axis=1))
                probs.append(jnp.concatenate([pc.astype(BF16) for pc in ps], axis=1))
            pv = jnp.dot(jnp.concatenate(probs, axis=0), ckv, preferred_element_type=F32)
            acc_ref[pl.ds(g0, grp), :] = acc_ref[pl.ds(g0, grp), :] * jnp.concatenate(alphas, axis=0) + pv
            return c2

        lax.fori_loop(0, rows_all // grp, group_body, 0, unroll=2)
        return carry

    lax.fori_loop(0, nkc, attn_body, 0)

    for h in range(n_heads):
        l = l_ref[h * tq:(h + 1) * tq, :]
        o_lat = (acc_ref[h * tq:(h + 1) * tq, :] / jnp.concatenate([l] * (d_lat // LANES), axis=1)).astype(BF16)
        o_ref[:, h * head_dim:(h + 1) * head_dim] = jnp.dot(
            o_lat, wuv_ref[h], preferred_element_type=F32).astype(o_ref.dtype)


def _attention(zh, zs, ckv, qlat, w_uv, bsz, seq, qi_col, n_idx, kw_col, tq, ts, slab, grp):
    t = zh.shape[0]
    nh, dl, hd = w_uv.shape
    nq = seq // tq
    k_sel = min(TOPK_MAX, seq // 4)
    assert ts >= k_sel and seq % ts == 0 and seq % tq == 0 and ts % tq == 0
    assert grp % tq == 0 and (nh * tq) % grp == 0 and tq % slab == 0 and slab % BF16_ROWS == 0
    assert tq % SCORE_ROWS == 0
    kern = functools.partial(
        _attn_kernel, tq=tq, ts=ts, seq=seq, n_idx=n_idx, n_heads=nh, d_lat=dl, head_dim=hd, k_sel=k_sel,
        idx_scale=float(IDX_DIM ** -0.5 * n_idx ** -0.5), sm_scale=float(hd ** -0.5), slab=slab, grp=grp)
    return pl.pallas_call(
        kern, grid=(bsz, nq),
        in_specs=[pl.BlockSpec((tq, n_idx * IDX_DIM), lambda b, i: (b * nq + i, qi_col)),
                  pl.BlockSpec((tq, LANES), lambda b, i: (b * nq + i, kw_col)),
                  pl.BlockSpec((seq, LANES), lambda b, i: (b, kw_col)),
                  pl.BlockSpec((seq, dl), lambda b, i: (b, 0)),
                  pl.BlockSpec((tq, nh * dl), lambda b, i: (b * nq + i, 0)),
                  pl.BlockSpec((nh, dl, hd), lambda b, i: (0, 0, 0))],
        out_specs=pl.BlockSpec((tq, nh * hd), lambda b, i: (b * nq + i, 0)),
        out_shape=jax.ShapeDtypeStruct((t, nh * hd), BF16),
        scratch_shapes=[pltpu.VMEM((tq, seq), jnp.int32),
                        pltpu.VMEM((tq, LANES), jnp.int32),
                        pltpu.VMEM((tq, ts), F32),
                        pltpu.VMEM((nh * tq, dl), BF16),
                        pltpu.VMEM((nh * tq, LANES), F32),
                        pltpu.VMEM((nh * tq, LANES), F32),
                        pltpu.VMEM((nh * tq, dl), F32)],
        compiler_params=_params(2), name="dsa_attention")(zh, zs, zs, ckv, qlat, w_uv)


def _merge_kernel(ya_ref, yb_ref, yc_ref, wa_ref, wb_ref, wc_ref, g0_ref, g1_ref, g2_ref, b_ref, o_ref):
    b = b_ref[...]
    out = None
    for k, (y_ref, w_ref, g_ref) in enumerate(((ya_ref, wa_ref, g0_ref), (yb_ref, wb_ref, g1_ref),
                                                (yc_ref, wc_ref, g2_ref))):
        gate = jax.nn.sigmoid(g_ref[...].astype(F32) + b[k:k + 1, :])
        term = gate * jnp.dot(y_ref[...], w_ref[...], preferred_element_type=F32)
        out = term if out is None else out + term
    o_ref[...] = out.astype(o_ref.dtype)


def _merge(ya, yb, yc, wa, wb, wc, zg, b_gate, tm, tn):
    t = ya.shape[0]
    d = wa.shape[1]
    nj = d // tn

    def yspec(y):
        return pl.BlockSpec((tm, y.shape[1]), lambda i, j: (i, 0))

    def wspec(w):
        return pl.BlockSpec((w.shape[0], tn), lambda i, j: (0, j))

    def gspec(k):
        return pl.BlockSpec((tm, tn), lambda i, j: (i, k * nj + j))

    return pl.pallas_call(
        _merge_kernel, grid=(t // tm, nj),
        in_specs=[yspec(ya), yspec(yb), yspec(yc), wspec(wa), wspec(wb), wspec(wc),
                  gspec(0), gspec(1), gspec(2), pl.BlockSpec((b_gate.shape[0], tn), lambda i, j: (0, j))],
        out_specs=pl.BlockSpec((tm, tn), lambda i, j: (i, j)),
        out_shape=jax.ShapeDtypeStruct((t, d), BF16),
        compiler_params=_params(2), name="merge")(ya, yb, yc, wa, wb, wc, zg, zg, zg, b_gate)


def _ln_kernel(y_ref, g_ref, b_ref, o_ref, ob_ref):
    y = y_ref[...]
    mu = jnp.mean(y, axis=-1, keepdims=True)
    yc = y - mu
    var = jnp.mean(yc * yc, axis=-1, keepdims=True)
    out = yc * lax.rsqrt(var + LN_EPS) * g_ref[...] + b_ref[...]
    o_ref[...] = out
    ob_ref[...] = out.astype(BF16)


def _layer_norm(y, g, b, rows):
    t, d = y.shape
    row_spec = pl.BlockSpec((rows, d), lambda i: (i, 0))
    vec_spec = pl.BlockSpec((1, d), lambda i: (0, 0))
    return pl.pallas_call(
        _ln_kernel, grid=(t // rows,), in_specs=[row_spec, vec_spec, vec_spec],
        out_specs=[row_spec, row_spec],
        out_shape=[jax.ShapeDtypeStruct((t, d), F32), jax.ShapeDtypeStruct((t, d), BF16)],
        compiler_params=_params(1), name="layer_norm")(y, g, b)


def _ffn_up_kernel(x_ref, wg_ref, wu_ref, cw_ref, o_ref, buf_ref, carry_ref, *, tm, tiles_per_seq):
    i = pl.program_id(0)
    j = pl.program_id(1)
    gt = jnp.dot(x_ref[...], wg_ref[...], preferred_element_type=F32)
    up = jnp.dot(x_ref[...], wu_ref[...], preferred_element_type=F32)
    seq_start = i % tiles_per_seq == 0

    @pl.when(seq_start)
    def _():
        buf_ref[0:SUBLANES, :] = jnp.zeros((SUBLANES, gt.shape[1]), F32)

    @pl.when(jnp.logical_not(seq_start))
    def _():
        buf_ref[0:SUBLANES, :] = carry_ref[j]

    buf_ref[SUBLANES:SUBLANES + tm, :] = gt
    carry_ref[j] = gt[tm - SUBLANES:tm, :]
    s1 = buf_ref[SUBLANES - 1:SUBLANES - 1 + tm, :]
    s2 = buf_ref[SUBLANES - 2:SUBLANES - 2 + tm, :]
    w = cw_ref[...]
    conv = w[0:1, :] * s2 + w[1:2, :] * s1 + w[2:3, :] * gt
    o_ref[...] = (jax.nn.silu(conv) * up).astype(o_ref.dtype)


def _ffn_up(xb, w_up, conv_w, seq, tm, tn):
    t, d = xb.shape
    dff = conv_w.shape[1]
    assert seq % tm == 0 and dff % tn == 0 and w_up.shape[1] == 2 * dff
    nj = dff // tn
    return pl.pallas_call(
        functools.partial(_ffn_up_kernel, tm=tm, tiles_per_seq=seq // tm), grid=(t // tm, nj),
        in_specs=[pl.BlockSpec((tm, d), lambda i, j: (i, 0)),
                  pl.BlockSpec((d, tn), lambda i, j: (0, j)),
                  pl.BlockSpec((d, tn), lambda i, j: (0, nj + j)),
                  pl.BlockSpec((CONV_W, tn), lambda i, j: (0, j))],
        out_specs=pl.BlockSpec((tm, tn), lambda i, j: (i, j)),
        out_shape=jax.ShapeDtypeStruct((t, dff), BF16),
        scratch_shapes=[pltpu.VMEM((SUBLANES + tm, tn), F32), pltpu.VMEM((nj, SUBLANES, tn), F32)],
        compiler_params=_params(2), name="ffn_up")(xb, w_up, w_up, conv_w)


def _tiles(seq, d_model):
    big = d_model >= 4096
    return dict(
        mm=(1024, 1024) if big else (256, 512),
        down=(512, 512) if big else (256, 512),
        ffn_up=(2048, 256) if big else (256, 256),
        merge=(1024, 512) if big else (256, 512),
        rows=512 if big else 256,
        ln_rows=256,
        tq=128, ts=512, slab=(32, 512))


def kernel(x, w_in, b_gate, conv_a, kv_norm, w_uk, w_uv, pool_w, pool_scale, w_br_a, w_br_b, w_br_c,
           w_o, ln1_g, ln1_b, w_up, conv_ffn_w, w_down, ln2_g, ln2_b):
    bsz, seq, d = x.shape
    depth = w_in.shape[0]
    alpha = (2.0 * depth) ** 0.25
    d_conv = conv_a.shape[2]
    n_heads, head_dim, d_lat = w_uk.shape[1:]
    d_att = n_heads * head_dim
    d_pool = pool_scale.shape[1]
    d_ff = conv_ffn_w.shape[2]
    n_in = w_in.shape[2]
    n_idx = (n_in - (3 * d_conv + d_att + d_lat + IDX_DIM + d_pool + b_gate.shape[1] * d)) // (IDX_DIM + 1)
    t = bsz * seq
    tl = _tiles(seq, d)

    o_bg, o_cg, o_v = 0, d_conv, 2 * d_conv
    o_q = 3 * d_conv
    o_ckv = o_q + d_att
    o_qi = o_ckv + d_lat
    o_ki = o_qi + n_idx * IDX_DIM
    o_wi = o_ki + IDX_DIM
    o_up = o_wi + n_idx
    o_g = o_up + d_pool
    assert o_g + b_gate.shape[1] * d == n_in
    assert IDX_DIM + n_idx <= LANES

    h_qi, h_q, h_a, h_up = 0, n_idx * IDX_DIM, n_idx * IDX_DIM + d_att, n_idx * IDX_DIM + d_att + 3 * d_conv
    n_h = h_up + d_pool
    assert h_q % head_dim == 0 and h_a % d_conv == 0 and h_up % d_pool == 0 and h_qi % (n_idx * IDX_DIM) == 0
    n_s = d_lat + LANES

    xf = x.reshape(t, d)
    xb = xf.astype(BF16)
    for l in range(depth):
        wl = w_in[l].astype(BF16)
        w_h = jnp.concatenate([wl[:, o_qi:o_ki], wl[:, o_q:o_ckv], wl[:, o_bg:o_q], wl[:, o_up:o_g]], axis=1)
        w_s = jnp.concatenate([wl[:, o_ckv:o_qi], wl[:, o_ki:o_up],
                               jnp.zeros((d, LANES - IDX_DIM - n_idx), BF16)], axis=1)
        w_g = wl[:, o_g:]

        zh = _matmul(xb, w_h, BF16, *tl["mm"], name="proj_wide")
        zs = _matmul(xb, w_s, F32, tl["mm"][0], n_s, name="proj_small")
        zg = _matmul(xb, w_g, BF16, *tl["mm"], name="proj_gates")

        ya = _conv_a(zh, conv_a[l], bsz, seq, h_a // d_conv, d_conv, tl["rows"])
        yc = _pool(zh, pool_w[l].astype(BF16), pool_scale[l].reshape(1, d_pool), bsz, seq, h_up // d_pool,
                   tl["rows"])
        ckv = _ckv(zs, kv_norm[l].reshape(1, d_lat), d_lat, tl["rows"])
        qlat = _qlat(zh, w_uk[l].astype(BF16), h_q // head_dim, tl["mm"][0])
        yb = _attention(zh, zs, ckv, qlat, w_uv[l].astype(BF16), bsz, seq, h_qi // (n_idx * IDX_DIM), n_idx,
                        d_lat // LANES, tl["tq"], min(tl["ts"], seq), *tl["slab"])

        merged = _merge(ya, yb, yc, w_br_a[l].astype(BF16), w_br_b[l].astype(BF16), w_br_c[l].astype(BF16),
                        zg, b_gate[l], *tl["merge"])
        y1 = _matmul(merged, w_o[l].astype(BF16), F32, *tl["mm"], res=xf, alpha=alpha, name="proj_out")
        xf, xb = _layer_norm(y1, ln1_g[l].reshape(1, d), ln1_b[l].reshape(1, d), tl["ln_rows"])

        h = _ffn_up(xb, w_up[l].astype(BF16), conv_ffn_w[l], seq, *tl["ffn_up"])
        y2 = _matmul(h, w_down[l].astype(BF16), F32, *tl["down"], res=xf, alpha=alpha, name="ffn_down")
        xf, xb = _layer_norm(y2, ln2_g[l].reshape(1, d), ln2_b[l].reshape(1, d), tl["ln_rows"])
    return xf.reshape(bsz, seq, d)
```

```python
import functools

import jax
import jax.numpy as jnp
from jax import lax
from jax.experimental import pallas as pl
from jax.experimental.pallas import tpu as pltpu

IDX_DIM = 64
TOPK_MAX = 256
POOL_WINDOWS = (2, 4, 8, 16)
CONV_W = 3
LN_EPS = 1e-5
RMS_EPS = 1e-6

LANES = 128
SUBLANES = 8
BF16_ROWS = 16
VMEM_BYTES_V7X = 64 * 1024 * 1024
VMEM_LIMIT = VMEM_BYTES_V7X - 8 * 1024 * 1024

HALO = BF16_ROWS
SCORE_ROWS = 128
FFN_SUB_ROWS = 256
NEG_BIG = -1e30
LOG2E = 1.4426950408889634
F32 = jnp.float32
BF16 = jnp.bfloat16


def _params(n_axes):
    return pltpu.CompilerParams(dimension_semantics=("arbitrary",) * n_axes,
                                vmem_limit_bytes=VMEM_LIMIT)


def _mm_kernel(x_ref, w_ref, o_ref):
    o_ref[...] = jnp.dot(x_ref[...], w_ref[...], preferred_element_type=F32).astype(o_ref.dtype)


def _mm_res_kernel(x_ref, w_ref, r_ref, o_ref, *, alpha):
    acc = jnp.dot(x_ref[...], w_ref[...], preferred_element_type=F32)
    o_ref[...] = (alpha * r_ref[...] + acc).astype(o_ref.dtype)


def _matmul(x, w, layer, out_dtype, tm, tn, res=None, alpha=None, name="mm"):
    m, k = x.shape
    n = w.shape[2]
    tm, tn = min(tm, m), min(tn, n)
    assert m % tm == 0 and n % tn == 0, (m, n, tm, tn)
    in_specs = [pl.BlockSpec((tm, k), lambda i, j: (i, 0)),
                pl.BlockSpec((None, k, tn), lambda i, j: (layer, 0, j))]
    args = [x, w]
    kern = _mm_kernel
    if res is not None:
        in_specs.append(pl.BlockSpec((tm, tn), lambda i, j: (i, j)))
        args.append(res)
        kern = functools.partial(_mm_res_kernel, alpha=alpha)
    return pl.pallas_call(
        kern, grid=(m // tm, n // tn), in_specs=in_specs,
        out_specs=pl.BlockSpec((tm, tn), lambda i, j: (i, j)),
        out_shape=jax.ShapeDtypeStruct((m, n), out_dtype),
        compiler_params=_params(2), name=name)(*args)


def _conv_a_kernel(bg_ref, cg_ref, v_ref, cgp_ref, vp_ref, w_ref, o_ref, buf_ref, *, rows):
    i = pl.program_id(1)
    u = cg_ref[...].astype(F32) * v_ref[...].astype(F32)
    up = cgp_ref[...].astype(F32) * vp_ref[...].astype(F32)
    buf_ref[0:HALO, :] = jnp.where(i > 0, up, 0.0)
    buf_ref[HALO:HALO + rows, :] = u
    s1 = buf_ref[HALO - 1:HALO - 1 + rows, :]
    s2 = buf_ref[HALO - 2:HALO - 2 + rows, :]
    w = w_ref[...]
    conv = w[0:1, :] * s2 + w[1:2, :] * s1 + w[2:3, :] * u
    o_ref[...] = (bg_ref[...].astype(F32) * conv).astype(o_ref.dtype)


def _conv_a(zh, conv_w, bsz, seq, col0, width, rows):
    t = zh.shape[0]
    nb = seq // rows
    hb = rows // HALO

    def cur(c):
        return pl.BlockSpec((rows, width), lambda b, i: (b * nb + i, col0 + c))

    def prev(c):
        return pl.BlockSpec((HALO, width), lambda b, i: (jnp.maximum((b * nb + i) * hb - 1, 0), col0 + c))

    return pl.pallas_call(
        functools.partial(_conv_a_kernel, rows=rows), grid=(bsz, nb),
        in_specs=[cur(0), cur(1), cur(2), prev(1), prev(2),
                  pl.BlockSpec((CONV_W, width), lambda b, i: (0, 0))],
        out_specs=pl.BlockSpec((rows, width), lambda b, i: (b * nb + i, 0)),
        out_shape=jax.ShapeDtypeStruct((t, width), BF16),
        scratch_shapes=[pltpu.VMEM((HALO + rows, width), F32)],
        compiler_params=_params(2), name="conv_a")(zh, zh, zh, zh, zh, conv_w)


def _pool_kernel(u_ref, up_ref, pw_ref, ps_ref, o_ref, buf_ref, *, rows, group):
    i = pl.program_id(1)
    ext = HALO + rows
    pos = i * rows + lax.broadcasted_iota(jnp.int32, (rows, 1), 0)
    buf_ref[0:HALO, :] = jnp.zeros((HALO, group), F32)
    for g, win in enumerate(POOL_WINDOWS):
        sl = slice(g * group, (g + 1) * group)
        u = u_ref[:, sl].astype(F32)
        buf_ref[HALO:2 * HALO, :] = jnp.where(i > 0, up_ref[:, sl].astype(F32), 0.0)
        buf_ref[2 * HALO:2 * HALO + rows, :] = u
        k = 1
        while k < win:
            s = buf_ref[HALO:HALO + ext, :] + buf_ref[HALO - k:HALO - k + ext, :]
            buf_ref[HALO:HALO + ext, :] = s
            k *= 2
        wsum = buf_ref[2 * HALO:2 * HALO + rows, :]
        cnt = jnp.minimum(pos + 1, win).astype(F32)
        d = wsum / cnt - u
        y = jnp.dot(d.astype(BF16), pw_ref[g], preferred_element_type=F32)
        o_ref[:, sl] = (y * ps_ref[:, sl]).astype(o_ref.dtype)


def _pool(zh, pool_w, pool_scale, bsz, seq, col0, rows):
    t = zh.shape[0]
    ngroups, group, _ = pool_w.shape
    width = ngroups * group
    nb = seq // rows
    hb = rows // HALO
    return pl.pallas_call(
        functools.partial(_pool_kernel, rows=rows, group=group), grid=(bsz, nb),
        in_specs=[pl.BlockSpec((rows, width), lambda b, i: (b * nb + i, col0)),
                  pl.BlockSpec((HALO, width), lambda b, i: (jnp.maximum((b * nb + i) * hb - 1, 0), col0)),
                  pl.BlockSpec((ngroups, group, group), lambda b, i: (0, 0, 0)),
                  pl.BlockSpec((1, width), lambda b, i: (0, 0))],
        out_specs=pl.BlockSpec((rows, width), lambda b, i: (b * nb + i, 0)),
        out_shape=jax.ShapeDtypeStruct((t, width), BF16),
        scratch_shapes=[pltpu.VMEM((2 * HALO + rows, group), F32)],
        compiler_params=_params(2), name="pool")(zh, zh, pool_w, pool_scale)


def _ckv_kernel(z_ref, g_ref, o_ref, *, d_lat):
    c = z_ref[:, 0:d_lat]
    ms = jnp.mean(c * c, axis=-1, keepdims=True)
    o_ref[...] = (c * lax.rsqrt(ms + RMS_EPS) * g_ref[...]).astype(o_ref.dtype)


def _ckv(zs, kv_norm, d_lat, rows):
    t, ns = zs.shape
    return pl.pallas_call(
        functools.partial(_ckv_kernel, d_lat=d_lat), grid=(t // rows,),
        in_specs=[pl.BlockSpec((rows, ns), lambda i: (i, 0)),
                  pl.BlockSpec((1, d_lat), lambda i: (0, 0))],
        out_specs=pl.BlockSpec((rows, d_lat), lambda i: (i, 0)),
        out_shape=jax.ShapeDtypeStruct((t, d_lat), BF16),
        compiler_params=_params(1), name="ckv_norm")(zs, kv_norm)


def _qlat_kernel(q_ref, w_ref, o_ref):
    o_ref[...] = jnp.dot(q_ref[...], w_ref[0], preferred_element_type=F32).astype(o_ref.dtype)


def _qlat(zh, w_uk, col0, rows):
    t = zh.shape[0]
    nh, hd, dl = w_uk.shape
    return pl.pallas_call(
        _qlat_kernel, grid=(t // rows, nh),
        in_specs=[pl.BlockSpec((rows, hd), lambda i, h: (i, col0 + h)),
                  pl.BlockSpec((1, hd, dl), lambda i, h: (h, 0, 0))],
        out_specs=pl.BlockSpec((rows, dl), lambda i, h: (i, h)),
        out_shape=jax.ShapeDtypeStruct((t, nh * dl), BF16),
        compiler_params=_params(2), name="q_lat")(zh, w_uk)


def _count_ge(key_ref, nkc, ts, tq, pred):
    def body(kc, part):
        m = pred(key_ref[:, pl.ds(pl.multiple_of(kc * ts, ts), ts)], kc).astype(F32)
        for c in range(ts // LANES):
            part = part + m[:, c * LANES:(c + 1) * LANES]
        return part

    part = lax.fori_loop(0, nkc, body, jnp.zeros((tq, LANES), F32))
    return jnp.sum(part, axis=1, keepdims=True)


def _bisect_largest(nbits, target, count_fn, tq):
    sign = jnp.int32(-2 ** 31) if nbits == 32 else jnp.int32(0)

    def body(b, v):
        bit = lax.shift_left(jnp.int32(1), jnp.int32(nbits - 1) - b)
        cand = v | bit
        cnt = count_fn(cand ^ sign)
        return jnp.where(cnt >= target, cand, v)

    v = lax.fori_loop(0, nbits, body, jnp.zeros((tq, 1), jnp.int32))
    return v ^ sign


def _attn_kernel(qi_ref, zq_ref, zk_ref, ckv_ref, ql_ref, wuv_ref, o_ref,
                 key_ref, cut_ref, bias_ref, qs_ref, m_ref, l_ref, acc_ref, *, tq, ts, seq, n_idx,
                 n_heads, d_lat, head_dim, k_sel, idx_scale, sm_scale, slab, grp):
    i = pl.program_id(1)
    q0 = i * tq
    nkc = (q0 + tq + ts - 1) // ts
    nbits_idx = max(1, (seq - 1).bit_length())
    row = q0 + lax.broadcasted_iota(jnp.int32, (tq, 1), 0)
    int_min = jnp.int32(-2 ** 31)

    wq = zq_ref[:, IDX_DIM:IDX_DIM + n_idx] * idx_scale

    def score_body(kc, carry):
        s0 = pl.multiple_of(kc * ts, ts)
        ki = zk_ref[pl.ds(s0, ts), 0:IDX_DIM].astype(BF16)
        col = s0 + lax.broadcasted_iota(jnp.int32, (SCORE_ROWS, ts), 1)
        for r0 in range(0, tq, SCORE_ROWS):
            acc = jnp.zeros((SCORE_ROWS, ts), F32)
            for h in range(n_idx):
                lg = lax.dot_general(qi_ref[r0:r0 + SCORE_ROWS, h * IDX_DIM:(h + 1) * IDX_DIM], ki,
                                     (((1,), (1,)), ((), ())), preferred_element_type=F32)
                acc = acc + wq[r0:r0 + SCORE_ROWS, h:h + 1] * jnp.maximum(lg, 0.0)
            bits = lax.bitcast_convert_type(acc, jnp.int32)
            key = bits ^ (lax.shift_right_arithmetic(bits, 31) & jnp.int32(0x7FFFFFFF))
            key_ref[r0:r0 + SCORE_ROWS, pl.ds(s0, ts)] = jnp.where(
                col <= row[r0:r0 + SCORE_ROWS], key, int_min)
        return carry

    lax.fori_loop(0, nkc, score_body, 0)

    kf = jnp.float32(k_sel)
    thr = _bisect_largest(32, kf, lambda v: _count_ge(key_ref, nkc, ts, tq, lambda x, kc: x >= v), tq)
    n_ge = _count_ge(key_ref, nkc, ts, tq, lambda x, kc: x >= thr)

    def rev_idx(kc):
        return (seq - 1) - (kc * ts + lax.broadcasted_iota(jnp.int32, (tq, ts), 1))

    cut_ref[...] = jnp.zeros((tq, LANES), jnp.int32)

    @pl.when(jnp.max(n_ge) > kf)
    def _():
        n_gt = _count_ge(key_ref, nkc, ts, tq, lambda x, kc: x > thr)
        need = kf - n_gt
        c = _bisect_largest(
            nbits_idx, need,
            lambda v: _count_ge(key_ref, nkc, ts, tq, lambda x, kc: (x == thr) & (rev_idx(kc) >= v)), tq)
        cut_ref[...] = jnp.broadcast_to(c, (tq, LANES))

    cut = cut_ref[:, 0:1]

    rows_all = n_heads * tq
    nlc = ts // LANES
    k2 = sm_scale * LOG2E
    m_ref[...] = jnp.full((rows_all, LANES), NEG_BIG, F32)
    l_ref[...] = jnp.zeros((rows_all, LANES), F32)
    acc_ref[...] = jnp.zeros((rows_all, d_lat), F32)
    for h in range(n_heads):
        qs_ref[h * tq:(h + 1) * tq, :] = ql_ref[:, h * d_lat:(h + 1) * d_lat]

    def attn_body(kc, carry):
        s0 = pl.multiple_of(kc * ts, ts)
        x = key_ref[:, pl.ds(s0, ts)]
        col = s0 + lax.broadcasted_iota(jnp.int32, (tq, ts), 1)
        sel = ((x > thr) | ((x == thr) & (rev_idx(kc) >= cut))) & (col <= row)
        bias_ref[...] = jnp.where(sel, 0.0, NEG_BIG)
        ckv = ckv_ref[pl.ds(s0, ts), :]

        def group_body(g, c2):
            g0 = pl.multiple_of(g * grp, grp)
            sc = lax.dot_general(qs_ref[pl.ds(g0, grp), :], ckv, (((1,), (1,)), ((), ())),
                                 preferred_element_type=F32)
            alphas, probs = [], []
            for s in range(grp // slab):
                r0, b0 = s * slab, (s * slab) % tq
                xs = [sc[r0:r0 + slab, c * LANES:(c + 1) * LANES] * k2
                      + bias_ref[b0:b0 + slab, c * LANES:(c + 1) * LANES] for c in range(nlc)]
                mx = functools.reduce(jnp.maximum, xs)
                m_old = m_ref[pl.ds(g0 + r0, slab), :]
                m_new = jnp.maximum(m_old, jnp.max(mx, axis=1, keepdims=True))
                alpha = jnp.exp2(m_old - m_new)
                ps = [jnp.exp2(xc - m_new) for xc in xs]
                psum = functools.reduce(lambda a, b: a + b, ps)
                l_ref[pl.ds(g0 + r0, slab), :] = (alpha * l_ref[pl.ds(g0 + r0, slab), :]
                                                  + jnp.sum(psum, axis=1, keepdims=True))
                m_ref[pl.ds(g0 + r0, slab), :] = m_new
                alphas.append(jnp.concatenate([alpha] * (d_lat // LANES), axis=1))
                probs.append(jnp.concatenate([pc.astype(BF16) for pc in ps], axis=1))
            pv = jnp.dot(jnp.concatenate(probs, axis=0), ckv, preferred_element_type=F32)
            acc_ref[pl.ds(g0, grp), :] = acc_ref[pl.ds(g0, grp), :] * jnp.concatenate(alphas, axis=0) + pv
            return c2

        lax.fori_loop(0, rows_all // grp, group_body, 0, unroll=2)
        return carry

    lax.fori_loop(0, nkc, attn_body, 0)

    for h in range(n_heads):
        l = l_ref[h * tq:(h + 1) * tq, :]
        o_lat = (acc_ref[h * tq:(h + 1) * tq, :] / jnp.concatenate([l] * (d_lat // LANES), axis=1)).astype(BF16)
        o_ref[:, h * head_dim:(h + 1) * head_dim] = jnp.dot(
            o_lat, wuv_ref[h], preferred_element_type=F32).astype(o_ref.dtype)


def _attention(zh, zs, ckv, qlat, w_uv, bsz, seq, qi_col, n_idx, kw_col, tq, ts, slab, grp):
    t = zh.shape[0]
    nh, dl, hd = w_uv.shape
    nq = seq // tq
    k_sel = min(TOPK_MAX, seq // 4)
    assert ts >= k_sel and seq % ts == 0 and seq % tq == 0 and ts % tq == 0
    assert grp % tq == 0 and (nh * tq) % grp == 0 and tq % slab == 0 and slab % BF16_ROWS == 0
    assert tq % SCORE_ROWS == 0
    kern = functools.partial(
        _attn_kernel, tq=tq, ts=ts, seq=seq, n_idx=n_idx, n_heads=nh, d_lat=dl, head_dim=hd, k_sel=k_sel,
        idx_scale=float(IDX_DIM ** -0.5 * n_idx ** -0.5), sm_scale=float(hd ** -0.5), slab=slab, grp=grp)
    return pl.pallas_call(
        kern, grid=(bsz, nq),
        in_specs=[pl.BlockSpec((tq, n_idx * IDX_DIM), lambda b, i: (b * nq + i, qi_col)),
                  pl.BlockSpec((tq, LANES), lambda b, i: (b * nq + i, kw_col)),
                  pl.BlockSpec((seq, LANES), lambda b, i: (b, kw_col)),
                  pl.BlockSpec((seq, dl), lambda b, i: (b, 0)),
                  pl.BlockSpec((tq, nh * dl), lambda b, i: (b * nq + i, 0)),
                  pl.BlockSpec((nh, dl, hd), lambda b, i: (0, 0, 0))],
        out_specs=pl.BlockSpec((tq, nh * hd), lambda b, i: (b * nq + i, 0)),
        out_shape=jax.ShapeDtypeStruct((t, nh * hd), BF16),
        scratch_shapes=[pltpu.VMEM((tq, seq), jnp.int32),
                        pltpu.VMEM((tq, LANES), jnp.int32),
                        pltpu.VMEM((tq, ts), F32),
                        pltpu.VMEM((nh * tq, dl), BF16),
                        pltpu.VMEM((nh * tq, LANES), F32),
                        pltpu.VMEM((nh * tq, LANES), F32),
                        pltpu.VMEM((nh * tq, dl), F32)],
        compiler_params=_params(2), name="dsa_attention")(zh, zs, zs, ckv, qlat, w_uv)


def _merge_kernel(ya_ref, yb_ref, yc_ref, wa_ref, wb_ref, wc_ref, g0_ref, g1_ref, g2_ref, b_ref, o_ref):
    b = b_ref[...]
    out = None
    for k, (y_ref, w_ref, g_ref) in enumerate(((ya_ref, wa_ref, g0_ref), (yb_ref, wb_ref, g1_ref),
                                                (yc_ref, wc_ref, g2_ref))):
        gate = jax.nn.sigmoid(g_ref[...].astype(F32) + b[k:k + 1, :])
        term = gate * jnp.dot(y_ref[...], w_ref[...], preferred_element_type=F32)
        out = term if out is None else out + term
    o_ref[...] = out.astype(o_ref.dtype)


def _merge(ya, yb, yc, wa, wb, wc, layer, zg, b_gate, tm, tn):
    t = ya.shape[0]
    d = wa.shape[2]
    nj = d // tn

    def yspec(y):
        return pl.BlockSpec((tm, y.shape[1]), lambda i, j: (i, 0))

    def wspec(w):
        return pl.BlockSpec((None, w.shape[1], tn), lambda i, j: (layer, 0, j))

    def gspec(k):
        return pl.BlockSpec((tm, tn), lambda i, j: (i, k * nj + j))

    return pl.pallas_call(
        _merge_kernel, grid=(t // tm, nj),
        in_specs=[yspec(ya), yspec(yb), yspec(yc), wspec(wa), wspec(wb), wspec(wc),
                  gspec(0), gspec(1), gspec(2), pl.BlockSpec((b_gate.shape[0], tn), lambda i, j: (0, j))],
        out_specs=pl.BlockSpec((tm, tn), lambda i, j: (i, j)),
        out_shape=jax.ShapeDtypeStruct((t, d), BF16),
        compiler_params=_params(2), name="merge")(ya, yb, yc, wa, wb, wc, zg, zg, zg, b_gate)


def _ln_kernel(y_ref, g_ref, b_ref, o_ref, ob_ref):
    y = y_ref[...]
    mu = jnp.mean(y, axis=-1, keepdims=True)
    yc = y - mu
    var = jnp.mean(yc * yc, axis=-1, keepdims=True)
    out = yc * lax.rsqrt(var + LN_EPS) * g_ref[...] + b_ref[...]
    o_ref[...] = out
    ob_ref[...] = out.astype(BF16)


def _layer_norm(y, g, b, rows):
    t, d = y.shape
    row_spec = pl.BlockSpec((rows, d), lambda i: (i, 0))
    vec_spec = pl.BlockSpec((1, d), lambda i: (0, 0))
    return pl.pallas_call(
        _ln_kernel, grid=(t // rows,), in_specs=[row_spec, vec_spec, vec_spec],
        out_specs=[row_spec, row_spec],
        out_shape=[jax.ShapeDtypeStruct((t, d), F32), jax.ShapeDtypeStruct((t, d), BF16)],
        compiler_params=_params(1), name="layer_norm")(y, g, b)


def _ffn_up_kernel(x_ref, wg_ref, wu_ref, cw_ref, o_ref, buf_ref, carry_ref, *, tm, sub, tiles_per_seq):
    i = pl.program_id(0)
    j = pl.program_id(1)
    seq_start = i % tiles_per_seq == 0

    @pl.when(seq_start)
    def _():
        buf_ref[0:SUBLANES, :] = jnp.zeros((SUBLANES, buf_ref.shape[1]), F32)

    @pl.when(jnp.logical_not(seq_start))
    def _():
        buf_ref[0:SUBLANES, :] = carry_ref[j]

    w = cw_ref[...]
    for r0 in range(0, tm, sub):
        xs = x_ref[r0:r0 + sub, :]
        gt = jnp.dot(xs, wg_ref[...], preferred_element_type=F32)
        up = jnp.dot(xs, wu_ref[...], preferred_element_type=F32)
        buf_ref[SUBLANES + r0:SUBLANES + r0 + sub, :] = gt
        if r0 + sub == tm:
            carry_ref[j] = gt[sub - SUBLANES:sub, :]
        s1 = buf_ref[SUBLANES - 1 + r0:SUBLANES - 1 + r0 + sub, :]
        s2 = buf_ref[SUBLANES - 2 + r0:SUBLANES - 2 + r0 + sub, :]
        conv = w[0:1, :] * s2 + w[1:2, :] * s1 + w[2:3, :] * gt
        o_ref[r0:r0 + sub, :] = (jax.nn.silu(conv) * up).astype(o_ref.dtype)


def _ffn_up(xb, w_up, layer, conv_w, seq, tm, tn):
    t, d = xb.shape
    dff = conv_w.shape[1]
    assert seq % tm == 0 and dff % tn == 0 and w_up.shape[2] == 2 * dff
    nj = dff // tn
    return pl.pallas_call(
        functools.partial(_ffn_up_kernel, tm=tm, sub=min(tm, FFN_SUB_ROWS), tiles_per_seq=seq // tm),
        grid=(t // tm, nj),
        in_specs=[pl.BlockSpec((tm, d), lambda i, j: (i, 0)),
                  pl.BlockSpec((None, d, tn), lambda i, j: (layer, 0, j)),
                  pl.BlockSpec((None, d, tn), lambda i, j: (layer, 0, nj + j)),
                  pl.BlockSpec((CONV_W, tn), lambda i, j: (0, j))],
        out_specs=pl.BlockSpec((tm, tn), lambda i, j: (i, j)),
        out_shape=jax.ShapeDtypeStruct((t, dff), BF16),
        scratch_shapes=[pltpu.VMEM((SUBLANES + tm, tn), F32), pltpu.VMEM((nj, SUBLANES, tn), F32)],
        compiler_params=_params(2), name="ffn_up")(xb, w_up, w_up, conv_w)


def _tiles(seq, d_model):
    big = d_model >= 4096
    return dict(
        mm=(1024, 1024) if big else (256, 512),
        down=(512, 512) if big else (256, 512),
        ffn_up=(2048, 256) if big else (256, 256),
        merge=(1024, 512) if big else (256, 512),
        rows=512 if big else 256,
        ln_rows=256,
        tq=128, ts=512, slab=(32, 512))


def kernel(x, w_in, b_gate, conv_a, kv_norm, w_uk, w_uv, pool_w, pool_scale, w_br_a, w_br_b, w_br_c,
           w_o, ln1_g, ln1_b, w_up, conv_ffn_w, w_down, ln2_g, ln2_b):
    bsz, seq, d = x.shape
    depth = w_in.shape[0]
    alpha = (2.0 * depth) ** 0.25
    d_conv = conv_a.shape[2]
    n_heads, head_dim, d_lat = w_uk.shape[1:]
    d_att = n_heads * head_dim
    d_pool = pool_scale.shape[1]
    d_ff = conv_ffn_w.shape[2]
    n_in = w_in.shape[2]
    n_idx = (n_in - (3 * d_conv + d_att + d_lat + IDX_DIM + d_pool + b_gate.shape[1] * d)) // (IDX_DIM + 1)
    t = bsz * seq
    tl = _tiles(seq, d)

    o_bg, o_cg, o_v = 0, d_conv, 2 * d_conv
    o_q = 3 * d_conv
    o_ckv = o_q + d_att
    o_qi = o_ckv + d_lat
    o_ki = o_qi + n_idx * IDX_DIM
    o_wi = o_ki + IDX_DIM
    o_up = o_wi + n_idx
    o_g = o_up + d_pool
    assert o_g + b_gate.shape[1] * d == n_in
    assert IDX_DIM + n_idx <= LANES

    h_qi, h_q, h_a, h_up = 0, n_idx * IDX_DIM, n_idx * IDX_DIM + d_att, n_idx * IDX_DIM + d_att + 3 * d_conv
    n_h = h_up + d_pool
    assert h_q % head_dim == 0 and h_a % d_conv == 0 and h_up % d_pool == 0 and h_qi % (n_idx * IDX_DIM) == 0
    n_s = d_lat + LANES

    wi = w_in.astype(BF16)
    w_h = jnp.concatenate([wi[:, :, o_qi:o_ki], wi[:, :, o_q:o_ckv], wi[:, :, o_bg:o_q], wi[:, :, o_up:o_g]], axis=2)
    w_s = jnp.concatenate([wi[:, :, o_ckv:o_qi], wi[:, :, o_ki:o_up],
                           jnp.zeros((depth, d, LANES - IDX_DIM - n_idx), BF16)], axis=2)
    w_g = wi[:, :, o_g:]
    wa_b, wb_b, wc_b = w_br_a.astype(BF16), w_br_b.astype(BF16), w_br_c.astype(BF16)
    wo_b, wup_b, wdn_b = w_o.astype(BF16), w_up.astype(BF16), w_down.astype(BF16)

    xf = x.reshape(t, d)
    xb = xf.astype(BF16)
    for l in range(depth):
        zh = _matmul(xb, w_h, l, BF16, *tl["mm"], name="proj_wide")
        zs = _matmul(xb, w_s, l, F32, tl["mm"][0], n_s, name="proj_small")
        zg = _matmul(xb, w_g, l, BF16, *tl["mm"], name="proj_gates")

        ya = _conv_a(zh, conv_a[l], bsz, seq, h_a // d_conv, d_conv, tl["rows"])
        yc = _pool(zh, pool_w[l].astype(BF16), pool_scale[l].reshape(1, d_pool), bsz, seq, h_up // d_pool,
                   tl["rows"])
        ckv = _ckv(zs, kv_norm[l].reshape(1, d_lat), d_lat, tl["rows"])
        qlat = _qlat(zh, w_uk[l].astype(BF16), h_q // head_dim, tl["mm"][0])
        yb = _attention(zh, zs, ckv, qlat, w_uv[l].astype(BF16), bsz, seq, h_qi // (n_idx * IDX_DIM), n_idx,
                        d_lat // LANES, tl["tq"], min(tl["ts"], seq), *tl["slab"])

        merged = _merge(ya, yb, yc, wa_b, wb_b, wc_b, l, zg, b_gate[l], *tl["merge"])
        y1 = _matmul(merged, wo_b, l, F32, *tl["mm"], res=xf, alpha=alpha, name="proj_out")
        xf, xb = _layer_norm(y1, ln1_g[l].reshape(1, d), ln1_b[l].reshape(1, d), tl["ln_rows"])

        h = _ffn_up(xb, wup_b, l, conv_ffn_w[l], seq, *tl["ffn_up"])
        y2 = _matmul(h, wdn_b, l, F32, *tl["down"], res=xf, alpha=alpha, name="ffn_down")
        xf, xb = _layer_norm(y2, ln2_g[l].reshape(1, d), ln2_b[l].reshape(1, d), tl["ln_rows"])
    return xf.reshape(bsz, seq, d)
```

```python
import functools

import jax
import jax.numpy as jnp
from jax import lax
from jax.experimental import pallas as pl
from jax.experimental.pallas import tpu as pltpu

IDX_DIM = 64
TOPK_MAX = 256
POOL_WINDOWS = (2, 4, 8, 16)
CONV_W = 3
LN_EPS = 1e-5
RMS_EPS = 1e-6

LANES = 128
SUBLANES = 8
BF16_ROWS = 16
VMEM_BYTES_V7X = 64 * 1024 * 1024
VMEM_LIMIT = VMEM_BYTES_V7X - 8 * 1024 * 1024

HALO = BF16_ROWS
SCORE_ROWS = 128
MERGE_SUB_ROWS = 256
FFN_SUB_ROWS = 256
NEG_BIG = -1e30
LOG2E = 1.4426950408889634
F32 = jnp.float32
BF16 = jnp.bfloat16


def _params(n_axes):
    return pltpu.CompilerParams(dimension_semantics=("arbitrary",) * n_axes,
                                vmem_limit_bytes=VMEM_LIMIT)


def _mm_kernel(x_ref, w_ref, o_ref):
    o_ref[...] = jnp.dot(x_ref[...], w_ref[...], preferred_element_type=F32).astype(o_ref.dtype)


def _mm_res_kernel(x_ref, w_ref, r_ref, o_ref, *, alpha):
    acc = jnp.dot(x_ref[...], w_ref[...], preferred_element_type=F32)
    o_ref[...] = (alpha * r_ref[...] + acc).astype(o_ref.dtype)


def _matmul(x, w, layer, out_dtype, tm, tn, res=None, alpha=None, name="mm"):
    m, k = x.shape
    n = w.shape[2]
    tm, tn = min(tm, m), min(tn, n)
    assert m % tm == 0 and n % tn == 0, (m, n, tm, tn)
    in_specs = [pl.BlockSpec((tm, k), lambda i, j: (i, 0)),
                pl.BlockSpec((None, k, tn), lambda i, j: (layer, 0, j))]
    args = [x, w]
    kern = _mm_kernel
    if res is not None:
        in_specs.append(pl.BlockSpec((tm, tn), lambda i, j: (i, j)))
        args.append(res)
        kern = functools.partial(_mm_res_kernel, alpha=alpha)
    return pl.pallas_call(
        kern, grid=(m // tm, n // tn), in_specs=in_specs,
        out_specs=pl.BlockSpec((tm, tn), lambda i, j: (i, j)),
        out_shape=jax.ShapeDtypeStruct((m, n), out_dtype),
        compiler_params=_params(2), name=name)(*args)


def _conv_a_kernel(bg_ref, cg_ref, v_ref, cgp_ref, vp_ref, w_ref, o_ref, buf_ref, *, rows):
    i = pl.program_id(1)
    u = cg_ref[...].astype(F32) * v_ref[...].astype(F32)
    up = cgp_ref[...].astype(F32) * vp_ref[...].astype(F32)
    buf_ref[0:HALO, :] = jnp.where(i > 0, up, 0.0)
    buf_ref[HALO:HALO + rows, :] = u
    s1 = buf_ref[HALO - 1:HALO - 1 + rows, :]
    s2 = buf_ref[HALO - 2:HALO - 2 + rows, :]
    w = w_ref[...]
    conv = w[0:1, :] * s2 + w[1:2, :] * s1 + w[2:3, :] * u
    o_ref[...] = (bg_ref[...].astype(F32) * conv).astype(o_ref.dtype)


def _conv_a(zh, conv_w, bsz, seq, col0, width, rows):
    t = zh.shape[0]
    nb = seq // rows
    hb = rows // HALO

    def cur(c):
        return pl.BlockSpec((rows, width), lambda b, i: (b * nb + i, col0 + c))

    def prev(c):
        return pl.BlockSpec((HALO, width), lambda b, i: (jnp.maximum((b * nb + i) * hb - 1, 0), col0 + c))

    return pl.pallas_call(
        functools.partial(_conv_a_kernel, rows=rows), grid=(bsz, nb),
        in_specs=[cur(0), cur(1), cur(2), prev(1), prev(2),
                  pl.BlockSpec((CONV_W, width), lambda b, i: (0, 0))],
        out_specs=pl.BlockSpec((rows, width), lambda b, i: (b * nb + i, 0)),
        out_shape=jax.ShapeDtypeStruct((t, width), BF16),
        scratch_shapes=[pltpu.VMEM((HALO + rows, width), F32)],
        compiler_params=_params(2), name="conv_a")(zh, zh, zh, zh, zh, conv_w)


def _pool_kernel(u_ref, up_ref, pw_ref, ps_ref, o_ref, buf_ref, *, rows, group):
    i = pl.program_id(1)
    ext = HALO + rows
    pos = i * rows + lax.broadcasted_iota(jnp.int32, (rows, 1), 0)
    buf_ref[0:HALO, :] = jnp.zeros((HALO, group), F32)
    for g, win in enumerate(POOL_WINDOWS):
        sl = slice(g * group, (g + 1) * group)
        u = u_ref[:, sl].astype(F32)
        buf_ref[HALO:2 * HALO, :] = jnp.where(i > 0, up_ref[:, sl].astype(F32), 0.0)
        buf_ref[2 * HALO:2 * HALO + rows, :] = u
        k = 1
        while k < win:
            s = buf_ref[HALO:HALO + ext, :] + buf_ref[HALO - k:HALO - k + ext, :]
            buf_ref[HALO:HALO + ext, :] = s
            k *= 2
        wsum = buf_ref[2 * HALO:2 * HALO + rows, :]
        cnt = jnp.minimum(pos + 1, win).astype(F32)
        d = wsum / cnt - u
        y = jnp.dot(d.astype(BF16), pw_ref[g], preferred_element_type=F32)
        o_ref[:, sl] = (y * ps_ref[:, sl]).astype(o_ref.dtype)


def _pool(zh, pool_w, pool_scale, bsz, seq, col0, rows):
    t = zh.shape[0]
    ngroups, group, _ = pool_w.shape
    width = ngroups * group
    nb = seq // rows
    hb = rows // HALO
    return pl.pallas_call(
        functools.partial(_pool_kernel, rows=rows, group=group), grid=(bsz, nb),
        in_specs=[pl.BlockSpec((rows, width), lambda b, i: (b * nb + i, col0)),
                  pl.BlockSpec((HALO, width), lambda b, i: (jnp.maximum((b * nb + i) * hb - 1, 0), col0)),
                  pl.BlockSpec((ngroups, group, group), lambda b, i: (0, 0, 0)),
                  pl.BlockSpec((1, width), lambda b, i: (0, 0))],
        out_specs=pl.BlockSpec((rows, width), lambda b, i: (b * nb + i, 0)),
        out_shape=jax.ShapeDtypeStruct((t, width), BF16),
        scratch_shapes=[pltpu.VMEM((2 * HALO + rows, group), F32)],
        compiler_params=_params(2), name="pool")(zh, zh, pool_w, pool_scale)


def _ckv_kernel(z_ref, g_ref, o_ref, *, d_lat):
    c = z_ref[:, 0:d_lat]
    ms = jnp.mean(c * c, axis=-1, keepdims=True)
    o_ref[...] = (c * lax.rsqrt(ms + RMS_EPS) * g_ref[...]).astype(o_ref.dtype)


def _ckv(zs, kv_norm, d_lat, rows):
    t, ns = zs.shape
    return pl.pallas_call(
        functools.partial(_ckv_kernel, d_lat=d_lat), grid=(t // rows,),
        in_specs=[pl.BlockSpec((rows, ns), lambda i: (i, 0)),
                  pl.BlockSpec((1, d_lat), lambda i: (0, 0))],
        out_specs=pl.BlockSpec((rows, d_lat), lambda i: (i, 0)),
        out_shape=jax.ShapeDtypeStruct((t, d_lat), BF16),
        compiler_params=_params(1), name="ckv_norm")(zs, kv_norm)


def _count_ge(key_ref, nkc, ts, tq, pred):
    def body(kc, part):
        m = pred(key_ref[:, pl.ds(pl.multiple_of(kc * ts, ts), ts)], kc).astype(F32)
        for c in range(ts // LANES):
            part = part + m[:, c * LANES:(c + 1) * LANES]
        return part

    part = lax.fori_loop(0, nkc, body, jnp.zeros((tq, LANES), F32))
    return jnp.sum(part, axis=1, keepdims=True)


def _bisect_largest(nbits, target, count_fn, tq):
    sign = jnp.int32(-2 ** 31) if nbits == 32 else jnp.int32(0)

    def body(b, v):
        bit = lax.shift_left(jnp.int32(1), jnp.int32(nbits - 1) - b)
        cand = v | bit
        cnt = count_fn(cand ^ sign)
        return jnp.where(cnt >= target, cand, v)

    v = lax.fori_loop(0, nbits, body, jnp.zeros((tq, 1), jnp.int32))
    return v ^ sign


def _attn_kernel(qi_ref, zq_ref, zk_ref, ckv_ref, q_ref, wuk_ref, wuv_ref, o_ref,
                 key_ref, cut_ref, bias_ref, qs_ref, m_ref, l_ref, acc_ref, *, tq, ts, seq, n_idx,
                 n_heads, d_lat, head_dim, k_sel, idx_scale, sm_scale, slab, grp):
    i = pl.program_id(1)
    q0 = i * tq
    nkc = (q0 + tq + ts - 1) // ts
    nbits_idx = max(1, (seq - 1).bit_length())
    row = q0 + lax.broadcasted_iota(jnp.int32, (tq, 1), 0)
    int_min = jnp.int32(-2 ** 31)

    wq = zq_ref[:, IDX_DIM:IDX_DIM + n_idx] * idx_scale

    def score_body(kc, carry):
        s0 = pl.multiple_of(kc * ts, ts)
        ki = zk_ref[pl.ds(s0, ts), 0:IDX_DIM].astype(BF16)
        col = s0 + lax.broadcasted_iota(jnp.int32, (SCORE_ROWS, ts), 1)
        for r0 in range(0, tq, SCORE_ROWS):
            acc = jnp.zeros((SCORE_ROWS, ts), F32)
            for h in range(n_idx):
                lg = lax.dot_general(qi_ref[r0:r0 + SCORE_ROWS, h * IDX_DIM:(h + 1) * IDX_DIM], ki,
                                     (((1,), (1,)), ((), ())), preferred_element_type=F32)
                acc = acc + wq[r0:r0 + SCORE_ROWS, h:h + 1] * jnp.maximum(lg, 0.0)
            bits = lax.bitcast_convert_type(acc, jnp.int32)
            key = bits ^ (lax.shift_right_arithmetic(bits, 31) & jnp.int32(0x7FFFFFFF))
            key_ref[r0:r0 + SCORE_ROWS, pl.ds(s0, ts)] = jnp.where(
                col <= row[r0:r0 + SCORE_ROWS], key, int_min)
        return carry

    lax.fori_loop(0, nkc, score_body, 0)

    kf = jnp.float32(k_sel)
    thr = _bisect_largest(32, kf, lambda v: _count_ge(key_ref, nkc, ts, tq, lambda x, kc: x >= v), tq)
    n_ge = _count_ge(key_ref, nkc, ts, tq, lambda x, kc: x >= thr)

    def rev_idx(kc):
        return (seq - 1) - (kc * ts + lax.broadcasted_iota(jnp.int32, (tq, ts), 1))

    cut_ref[...] = jnp.zeros((tq, LANES), jnp.int32)

    @pl.when(jnp.max(n_ge) > kf)
    def _():
        n_gt = _count_ge(key_ref, nkc, ts, tq, lambda x, kc: x > thr)
        need = kf - n_gt
        c = _bisect_largest(
            nbits_idx, need,
            lambda v: _count_ge(key_ref, nkc, ts, tq, lambda x, kc: (x == thr) & (rev_idx(kc) >= v)), tq)
        cut_ref[...] = jnp.broadcast_to(c, (tq, LANES))

    cut = cut_ref[:, 0:1]

    rows_all = n_heads * tq
    nlc = ts // LANES
    k2 = sm_scale * LOG2E
    m_ref[...] = jnp.full((rows_all, LANES), NEG_BIG, F32)
    l_ref[...] = jnp.zeros((rows_all, LANES), F32)
    acc_ref[...] = jnp.zeros((rows_all, d_lat), F32)
    for h in range(n_heads):
        qs_ref[h * tq:(h + 1) * tq, :] = jnp.dot(q_ref[:, h * head_dim:(h + 1) * head_dim], wuk_ref[h],
                                                 preferred_element_type=F32).astype(BF16)

    def logits(g0, ckv, bias_at):
        sc = lax.dot_general(qs_ref[pl.ds(g0, grp), :], ckv, (((1,), (1,)), ((), ())),
                             preferred_element_type=F32)
        out = []
        for s in range(grp // slab):
            r0, b0 = s * slab, (s * slab) % tq
            out.append([sc[r0:r0 + slab, c * LANES:(c + 1) * LANES] * k2 + bias_at(b0, c) for c in range(nlc)])
        return out

    def max_body(kc, carry):
        s0 = pl.multiple_of(kc * ts, ts)
        x = key_ref[:, pl.ds(s0, ts)]
        col = s0 + lax.broadcasted_iota(jnp.int32, (tq, ts), 1)
        sel = ((x > thr) | ((x == thr) & (rev_idx(kc) >= cut))) & (col <= row)
        bias = jnp.where(sel, 0.0, NEG_BIG)
        bias_ref[...] = bias
        key_ref[:, pl.ds(s0, ts)] = lax.bitcast_convert_type(bias, jnp.int32)
        ckv = ckv_ref[pl.ds(s0, ts), :]

        def group_body(g, c2):
            g0 = pl.multiple_of(g * grp, grp)
            xs_all = logits(g0, ckv, lambda b0, c: bias_ref[b0:b0 + slab, c * LANES:(c + 1) * LANES])
            for s, xs in enumerate(xs_all):
                rows_s = pl.ds(g0 + s * slab, slab)
                m_ref[rows_s, :] = jnp.maximum(m_ref[rows_s, :], functools.reduce(jnp.maximum, xs))
            return c2

        lax.fori_loop(0, rows_all // grp, group_body, 0, unroll=True)
        return carry

    lax.fori_loop(0, nkc, max_body, 0)
    m_ref[...] = jnp.broadcast_to(jnp.max(m_ref[...], axis=1, keepdims=True), (rows_all, LANES))

    def sum_body(kc, carry):
        s0 = pl.multiple_of(kc * ts, ts)
        bias_ref[...] = lax.bitcast_convert_type(key_ref[:, pl.ds(s0, ts)], F32)
        ckv = ckv_ref[pl.ds(s0, ts), :]

        def group_body(g, c2):
            g0 = pl.multiple_of(g * grp, grp)
            xs_all = logits(g0, ckv, lambda b0, c: bias_ref[b0:b0 + slab, c * LANES:(c + 1) * LANES])
            probs = []
            for s, xs in enumerate(xs_all):
                rows_s = pl.ds(g0 + s * slab, slab)
                m = m_ref[rows_s, :]
                ps = [jnp.exp2(xc - m) for xc in xs]
                l_ref[rows_s, :] = l_ref[rows_s, :] + functools.reduce(lambda a, b: a + b, ps)
                probs.append(jnp.concatenate([pc.astype(BF16) for pc in ps], axis=1))
            pv = jnp.dot(jnp.concatenate(probs, axis=0), ckv, preferred_element_type=F32)
            acc_ref[pl.ds(g0, grp), :] = acc_ref[pl.ds(g0, grp), :] + pv
            return c2

        lax.fori_loop(0, rows_all // grp, group_body, 0, unroll=True)
        return carry

    lax.fori_loop(0, nkc, sum_body, 0)

    for h in range(n_heads):
        l = jnp.sum(l_ref[h * tq:(h + 1) * tq, :], axis=1, keepdims=True)
        o_lat = (acc_ref[h * tq:(h + 1) * tq, :] / l).astype(BF16)
        o_ref[:, h * head_dim:(h + 1) * head_dim] = jnp.dot(
            o_lat, wuv_ref[h], preferred_element_type=F32).astype(o_ref.dtype)


def _attention(zh, zs, ckv, w_uk, w_uv, bsz, seq, qi_col, q_col, n_idx, kw_col, tq, ts, slab, grp):
    t = zh.shape[0]
    nh, dl, hd = w_uv.shape
    nq = seq // tq
    k_sel = min(TOPK_MAX, seq // 4)
    assert ts >= k_sel and seq % ts == 0 and seq % tq == 0 and ts % tq == 0
    assert grp % tq == 0 and (nh * tq) % grp == 0 and tq % slab == 0 and slab % BF16_ROWS == 0
    assert tq % SCORE_ROWS == 0
    kern = functools.partial(
        _attn_kernel, tq=tq, ts=ts, seq=seq, n_idx=n_idx, n_heads=nh, d_lat=dl, head_dim=hd, k_sel=k_sel,
        idx_scale=float(IDX_DIM ** -0.5 * n_idx ** -0.5), sm_scale=float(hd ** -0.5), slab=slab, grp=grp)
    return pl.pallas_call(
        kern, grid=(bsz, nq),
        in_specs=[pl.BlockSpec((tq, n_idx * IDX_DIM), lambda b, i: (b * nq + i, qi_col)),
                  pl.BlockSpec((tq, LANES), lambda b, i: (b * nq + i, kw_col)),
                  pl.BlockSpec((seq, LANES), lambda b, i: (b, kw_col)),
                  pl.BlockSpec((seq, dl), lambda b, i: (b, 0)),
                  pl.BlockSpec((tq, nh * hd), lambda b, i: (b * nq + i, q_col)),
                  pl.BlockSpec((nh, hd, dl), lambda b, i: (0, 0, 0)),
                  pl.BlockSpec((nh, dl, hd), lambda b, i: (0, 0, 0))],
        out_specs=pl.BlockSpec((tq, nh * hd), lambda b, i: (b * nq + i, 0)),
        out_shape=jax.ShapeDtypeStruct((t, nh * hd), BF16),
        scratch_shapes=[pltpu.VMEM((tq, seq), jnp.int32),
                        pltpu.VMEM((tq, LANES), jnp.int32),
                        pltpu.VMEM((tq, ts), F32),
                        pltpu.VMEM((nh * tq, dl), BF16),
                        pltpu.VMEM((nh * tq, LANES), F32),
                        pltpu.VMEM((nh * tq, LANES), F32),
                        pltpu.VMEM((nh * tq, dl), F32)],
        compiler_params=_params(2), name="dsa_attention")(zh, zs, zs, ckv, zh, w_uk, w_uv)


def _merge_kernel(ya_ref, yb_ref, yc_ref, wa_ref, wb_ref, wc_ref, g0_ref, g1_ref, g2_ref, b_ref, o_ref):
    b = b_ref[...]
    tm = o_ref.shape[0]
    sub = min(tm, MERGE_SUB_ROWS)
    for r0 in range(0, tm, sub):
        out = None
        for k, (y_ref, w_ref, g_ref) in enumerate(((ya_ref, wa_ref, g0_ref), (yb_ref, wb_ref, g1_ref),
                                                    (yc_ref, wc_ref, g2_ref))):
            gate = jax.nn.sigmoid(g_ref[r0:r0 + sub, :].astype(F32) + b[k:k + 1, :])
            term = gate * jnp.dot(y_ref[r0:r0 + sub, :], w_ref[...], preferred_element_type=F32)
            out = term if out is None else out + term
        o_ref[r0:r0 + sub, :] = out.astype(o_ref.dtype)


def _merge(ya, yb, yc, wa, wb, wc, layer, zg, b_gate, tm, tn):
    t = ya.shape[0]
    d = wa.shape[2]
    nj = d // tn

    def yspec(y):
        return pl.BlockSpec((tm, y.shape[1]), lambda i, j: (i, 0))

    def wspec(w):
        return pl.BlockSpec((None, w.shape[1], tn), lambda i, j: (layer, 0, j))

    def gspec(k):
        return pl.BlockSpec((tm, tn), lambda i, j: (i, k * nj + j))

    return pl.pallas_call(
        _merge_kernel, grid=(t // tm, nj),
        in_specs=[yspec(ya), yspec(yb), yspec(yc), wspec(wa), wspec(wb), wspec(wc),
                  gspec(0), gspec(1), gspec(2), pl.BlockSpec((b_gate.shape[0], tn), lambda i, j: (0, j))],
        out_specs=pl.BlockSpec((tm, tn), lambda i, j: (i, j)),
        out_shape=jax.ShapeDtypeStruct((t, d), BF16),
        compiler_params=_params(2), name="merge")(ya, yb, yc, wa, wb, wc, zg, zg, zg, b_gate)


def _ln_kernel(y_ref, g_ref, b_ref, o_ref, ob_ref):
    y = y_ref[...]
    mu = jnp.mean(y, axis=-1, keepdims=True)
    yc = y - mu
    var = jnp.mean(yc * yc, axis=-1, keepdims=True)
    out = yc * lax.rsqrt(var + LN_EPS) * g_ref[...] + b_ref[...]
    o_ref[...] = out
    ob_ref[...] = out.astype(BF16)


def _layer_norm(y, g, b, rows):
    t, d = y.shape
    row_spec = pl.BlockSpec((rows, d), lambda i: (i, 0))
    vec_spec = pl.BlockSpec((1, d), lambda i: (0, 0))
    return pl.pallas_call(
        _ln_kernel, grid=(t // rows,), in_specs=[row_spec, vec_spec, vec_spec],
        out_specs=[row_spec, row_spec],
        out_shape=[jax.ShapeDtypeStruct((t, d), F32), jax.ShapeDtypeStruct((t, d), BF16)],
        compiler_params=_params(1), name="layer_norm")(y, g, b)


def _ffn_up_kernel(x_ref, wg_ref, wu_ref, cw_ref, o_ref, buf_ref, carry_ref, *, tm, sub, tiles_per_seq):
    i = pl.program_id(0)
    j = pl.program_id(1)
    seq_start = i % tiles_per_seq == 0

    @pl.when(seq_start)
    def _():
        buf_ref[0:SUBLANES, :] = jnp.zeros((SUBLANES, buf_ref.shape[1]), F32)

    @pl.when(jnp.logical_not(seq_start))
    def _():
        buf_ref[0:SUBLANES, :] = carry_ref[j]

    w = cw_ref[...]
    for r0 in range(0, tm, sub):
        xs = x_ref[r0:r0 + sub, :]
        gt = jnp.dot(xs, wg_ref[...], preferred_element_type=F32)
        up = jnp.dot(xs, wu_ref[...], preferred_element_type=F32)
        buf_ref[SUBLANES + r0:SUBLANES + r0 + sub, :] = gt
        if r0 + sub == tm:
            carry_ref[j] = gt[sub - SUBLANES:sub, :]
        s1 = buf_ref[SUBLANES - 1 + r0:SUBLANES - 1 + r0 + sub, :]
        s2 = buf_ref[SUBLANES - 2 + r0:SUBLANES - 2 + r0 + sub, :]
        conv = w[0:1, :] * s2 + w[1:2, :] * s1 + w[2:3, :] * gt
        o_ref[r0:r0 + sub, :] = (jax.nn.silu(conv) * up).astype(o_ref.dtype)


def _ffn_up(xb, w_up, layer, conv_w, seq, tm, tn):
    t, d = xb.shape
    dff = conv_w.shape[1]
    assert seq % tm == 0 and dff % tn == 0 and w_up.shape[2] == 2 * dff
    nj = dff // tn
    return pl.pallas_call(
        functools.partial(_ffn_up_kernel, tm=tm, sub=min(tm, FFN_SUB_ROWS), tiles_per_seq=seq // tm),
        grid=(t // tm, nj),
        in_specs=[pl.BlockSpec((tm, d), lambda i, j: (i, 0)),
                  pl.BlockSpec((None, d, tn), lambda i, j: (layer, 0, j)),
                  pl.BlockSpec((None, d, tn), lambda i, j: (layer, 0, nj + j)),
                  pl.BlockSpec((CONV_W, tn), lambda i, j: (0, j))],
        out_specs=pl.BlockSpec((tm, tn), lambda i, j: (i, j)),
        out_shape=jax.ShapeDtypeStruct((t, dff), BF16),
        scratch_shapes=[pltpu.VMEM((SUBLANES + tm, tn), F32), pltpu.VMEM((nj, SUBLANES, tn), F32)],
        compiler_params=_params(2), name="ffn_up")(xb, w_up, w_up, conv_w)


def _tiles(seq, d_model):
    big = d_model >= 4096
    return dict(
        mm=(1024, 1024) if big else (256, 512),
        down=(512, 512) if big else (256, 512),
        ffn_up=(2048, 256) if big else (256, 256),
        merge=(1024, 512) if big else (256, 512),
        rows=512 if big else 256,
        ln_rows=256,
        tq=128, ts=512, slab=(32, 512))


def kernel(x, w_in, b_gate, conv_a, kv_norm, w_uk, w_uv, pool_w, pool_scale, w_br_a, w_br_b, w_br_c,
           w_o, ln1_g, ln1_b, w_up, conv_ffn_w, w_down, ln2_g, ln2_b):
    bsz, seq, d = x.shape
    depth = w_in.shape[0]
    alpha = (2.0 * depth) ** 0.25
    d_conv = conv_a.shape[2]
    n_heads, head_dim, d_lat = w_uk.shape[1:]
    d_att = n_heads * head_dim
    d_pool = pool_scale.shape[1]
    d_ff = conv_ffn_w.shape[2]
    n_in = w_in.shape[2]
    n_idx = (n_in - (3 * d_conv + d_att + d_lat + IDX_DIM + d_pool + b_gate.shape[1] * d)) // (IDX_DIM + 1)
    t = bsz * seq
    tl = _tiles(seq, d)

    o_bg, o_cg, o_v = 0, d_conv, 2 * d_conv
    o_q = 3 * d_conv
    o_ckv = o_q + d_att
    o_qi = o_ckv + d_lat
    o_ki = o_qi + n_idx * IDX_DIM
    o_wi = o_ki + IDX_DIM
    o_up = o_wi + n_idx
    o_g = o_up + d_pool
    assert o_g + b_gate.shape[1] * d == n_in
    assert IDX_DIM + n_idx <= LANES

    h_qi, h_q, h_a, h_up = 0, n_idx * IDX_DIM, n_idx * IDX_DIM + d_att, n_idx * IDX_DIM + d_att + 3 * d_conv
    n_h = h_up + d_pool
    assert h_q % d_att == 0 and h_a % d_conv == 0 and h_up % d_pool == 0 and h_qi % (n_idx * IDX_DIM) == 0
    n_s = d_lat + LANES

    wi = w_in.astype(BF16)
    w_h = jnp.concatenate([wi[:, :, o_qi:o_ki], wi[:, :, o_q:o_ckv], wi[:, :, o_bg:o_q], wi[:, :, o_up:o_g]], axis=2)
    w_s = jnp.concatenate([wi[:, :, o_ckv:o_qi], wi[:, :, o_ki:o_up],
                           jnp.zeros((depth, d, LANES - IDX_DIM - n_idx), BF16)], axis=2)
    w_g = wi[:, :, o_g:]
    wa_b, wb_b, wc_b = w_br_a.astype(BF16), w_br_b.astype(BF16), w_br_c.astype(BF16)
    wo_b, wup_b, wdn_b = w_o.astype(BF16), w_up.astype(BF16), w_down.astype(BF16)

    xf = x.reshape(t, d)
    xb = xf.astype(BF16)
    for l in range(depth):
        zh = _matmul(xb, w_h, l, BF16, *tl["mm"], name="proj_wide")
        zs = _matmul(xb, w_s, l, F32, tl["mm"][0], n_s, name="proj_small")
        zg = _matmul(xb, w_g, l, BF16, *tl["mm"], name="proj_gates")

        ya = _conv_a(zh, conv_a[l], bsz, seq, h_a // d_conv, d_conv, tl["rows"])
        yc = _pool(zh, pool_w[l].astype(BF16), pool_scale[l].reshape(1, d_pool), bsz, seq, h_up // d_pool,
                   tl["rows"])
        ckv = _ckv(zs, kv_norm[l].reshape(1, d_lat), d_lat, tl["rows"])
        yb = _attention(zh, zs, ckv, w_uk[l].astype(BF16), w_uv[l].astype(BF16), bsz, seq,
                        h_qi // (n_idx * IDX_DIM), h_q // d_att, n_idx, d_lat // LANES,
                        tl["tq"], min(tl["ts"], seq), *tl["slab"])

        merged = _merge(ya, yb, yc, wa_b, wb_b, wc_b, l, zg, b_gate[l], *tl["merge"])
        y1 = _matmul(merged, wo_b, l, F32, *tl["mm"], res=xf, alpha=alpha, name="proj_out")
        xf, xb = _layer_norm(y1, ln1_g[l].reshape(1, d), ln1_b[l].reshape(1, d), tl["ln_rows"])

        h = _ffn_up(xb, wup_b, l, conv_ffn_w[l], seq, *tl["ffn_up"])
        y2 = _matmul(h, wdn_b, l, F32, *tl["down"], res=xf, alpha=alpha, name="ffn_down")
        xf, xb = _layer_norm(y2, ln2_g[l].reshape(1, d), ln2_b[l].reshape(1, d), tl["ln_rows"])
    return xf.reshape(bsz, seq, d)
```

```python
import functools

import jax
import jax.numpy as jnp
from jax import lax
from jax.experimental import pallas as pl
from jax.experimental.pallas import tpu as pltpu

IDX_DIM = 64
TOPK_MAX = 256
POOL_WINDOWS = (2, 4, 8, 16)
CONV_W = 3
LN_EPS = 1e-5
RMS_EPS = 1e-6

LANES = 128
SUBLANES = 8
BF16_ROWS = 16
VMEM_BYTES_V7X = 64 * 1024 * 1024
VMEM_LIMIT = VMEM_BYTES_V7X - 8 * 1024 * 1024

HALO = BF16_ROWS
SCORE_ROWS = 128
GROUP_UNROLL = 4
MERGE_SUB_ROWS = 256
FFN_SUB_ROWS = 256
NEG_BIG = -1e30
LOG2E = 1.4426950408889634
F32 = jnp.float32
BF16 = jnp.bfloat16


def _params(n_axes):
    return pltpu.CompilerParams(dimension_semantics=("arbitrary",) * n_axes,
                                vmem_limit_bytes=VMEM_LIMIT)


def _mm_kernel(x_ref, w_ref, o_ref):
    o_ref[...] = jnp.dot(x_ref[...], w_ref[...], preferred_element_type=F32).astype(o_ref.dtype)


def _mm_res_kernel(x_ref, w_ref, r_ref, o_ref, *, alpha):
    acc = jnp.dot(x_ref[...], w_ref[...], preferred_element_type=F32)
    o_ref[...] = (alpha * r_ref[...] + acc).astype(o_ref.dtype)


def _matmul(x, w, layer, out_dtype, tm, tn, res=None, alpha=None, name="mm"):
    m, k = x.shape
    n = w.shape[2]
    tm, tn = min(tm, m), min(tn, n)
    assert m % tm == 0 and n % tn == 0, (m, n, tm, tn)
    in_specs = [pl.BlockSpec((tm, k), lambda i, j: (i, 0)),
                pl.BlockSpec((None, k, tn), lambda i, j: (layer, 0, j))]
    args = [x, w]
    kern = _mm_kernel
    if res is not None:
        in_specs.append(pl.BlockSpec((tm, tn), lambda i, j: (i, j)))
        args.append(res)
        kern = functools.partial(_mm_res_kernel, alpha=alpha)
    return pl.pallas_call(
        kern, grid=(m // tm, n // tn), in_specs=in_specs,
        out_specs=pl.BlockSpec((tm, tn), lambda i, j: (i, j)),
        out_shape=jax.ShapeDtypeStruct((m, n), out_dtype),
        compiler_params=_params(2), name=name)(*args)


def _conv_a_kernel(bg_ref, cg_ref, v_ref, cgp_ref, vp_ref, w_ref, o_ref, buf_ref, *, rows):
    i = pl.program_id(1)
    u = cg_ref[...].astype(F32) * v_ref[...].astype(F32)
    up = cgp_ref[...].astype(F32) * vp_ref[...].astype(F32)
    buf_ref[0:HALO, :] = jnp.where(i > 0, up, 0.0)
    buf_ref[HALO:HALO + rows, :] = u
    s1 = buf_ref[HALO - 1:HALO - 1 + rows, :]
    s2 = buf_ref[HALO - 2:HALO - 2 + rows, :]
    w = w_ref[...]
    conv = w[0:1, :] * s2 + w[1:2, :] * s1 + w[2:3, :] * u
    o_ref[...] = (bg_ref[...].astype(F32) * conv).astype(o_ref.dtype)


def _conv_a(zh, conv_w, bsz, seq, col0, width, rows):
    t = zh.shape[0]
    nb = seq // rows
    hb = rows // HALO

    def cur(c):
        return pl.BlockSpec((rows, width), lambda b, i: (b * nb + i, col0 + c))

    def prev(c):
        return pl.BlockSpec((HALO, width), lambda b, i: (jnp.maximum((b * nb + i) * hb - 1, 0), col0 + c))

    return pl.pallas_call(
        functools.partial(_conv_a_kernel, rows=rows), grid=(bsz, nb),
        in_specs=[cur(0), cur(1), cur(2), prev(1), prev(2),
                  pl.BlockSpec((CONV_W, width), lambda b, i: (0, 0))],
        out_specs=pl.BlockSpec((rows, width), lambda b, i: (b * nb + i, 0)),
        out_shape=jax.ShapeDtypeStruct((t, width), BF16),
        scratch_shapes=[pltpu.VMEM((HALO + rows, width), F32)],
        compiler_params=_params(2), name="conv_a")(zh, zh, zh, zh, zh, conv_w)


def _pool_kernel(u_ref, up_ref, pw_ref, ps_ref, o_ref, buf_ref, *, rows, group):
    i = pl.program_id(1)
    ext = HALO + rows
    pos = i * rows + lax.broadcasted_iota(jnp.int32, (rows, 1), 0)
    buf_ref[0:HALO, :] = jnp.zeros((HALO, group), F32)
    for g, win in enumerate(POOL_WINDOWS):
        sl = slice(g * group, (g + 1) * group)
        u = u_ref[:, sl].astype(F32)
        buf_ref[HALO:2 * HALO, :] = jnp.where(i > 0, up_ref[:, sl].astype(F32), 0.0)
        buf_ref[2 * HALO:2 * HALO + rows, :] = u
        k = 1
        while k < win:
            s = buf_ref[HALO:HALO + ext, :] + buf_ref[HALO - k:HALO - k + ext, :]
            buf_ref[HALO:HALO + ext, :] = s
            k *= 2
        wsum = buf_ref[2 * HALO:2 * HALO + rows, :]
        cnt = jnp.minimum(pos + 1, win).astype(F32)
        d = wsum / cnt - u
        y = jnp.dot(d.astype(BF16), pw_ref[g], preferred_element_type=F32)
        o_ref[:, sl] = (y * ps_ref[:, sl]).astype(o_ref.dtype)


def _pool(zh, pool_w, pool_scale, bsz, seq, col0, rows):
    t = zh.shape[0]
    ngroups, group, _ = pool_w.shape
    width = ngroups * group
    nb = seq // rows
    hb = rows // HALO
    return pl.pallas_call(
        functools.partial(_pool_kernel, rows=rows, group=group), grid=(bsz, nb),
        in_specs=[pl.BlockSpec((rows, width), lambda b, i: (b * nb + i, col0)),
                  pl.BlockSpec((HALO, width), lambda b, i: (jnp.maximum((b * nb + i) * hb - 1, 0), col0)),
                  pl.BlockSpec((ngroups, group, group), lambda b, i: (0, 0, 0)),
                  pl.BlockSpec((1, width), lambda b, i: (0, 0))],
        out_specs=pl.BlockSpec((rows, width), lambda b, i: (b * nb + i, 0)),
        out_shape=jax.ShapeDtypeStruct((t, width), BF16),
        scratch_shapes=[pltpu.VMEM((2 * HALO + rows, group), F32)],
        compiler_params=_params(2), name="pool")(zh, zh, pool_w, pool_scale)


def _ckv_kernel(z_ref, g_ref, o_ref, *, d_lat):
    c = z_ref[:, 0:d_lat]
    ms = jnp.mean(c * c, axis=-1, keepdims=True)
    o_ref[...] = (c * lax.rsqrt(ms + RMS_EPS) * g_ref[...]).astype(o_ref.dtype)


def _ckv(zs, kv_norm, d_lat, rows):
    t, ns = zs.shape
    return pl.pallas_call(
        functools.partial(_ckv_kernel, d_lat=d_lat), grid=(t // rows,),
        in_specs=[pl.BlockSpec((rows, ns), lambda i: (i, 0)),
                  pl.BlockSpec((1, d_lat), lambda i: (0, 0))],
        out_specs=pl.BlockSpec((rows, d_lat), lambda i: (i, 0)),
        out_shape=jax.ShapeDtypeStruct((t, d_lat), BF16),
        compiler_params=_params(1), name="ckv_norm")(zs, kv_norm)


def _count_ge(key_ref, nkc, ts, tq, pred):
    def body(kc, part):
        m = pred(key_ref[:, pl.ds(pl.multiple_of(kc * ts, ts), ts)], kc).astype(F32)
        for c in range(ts // LANES):
            part = part + m[:, c * LANES:(c + 1) * LANES]
        return part

    part = lax.fori_loop(0, nkc, body, jnp.zeros((tq, LANES), F32))
    return jnp.sum(part, axis=1, keepdims=True)


def _bisect_largest(nbits, target, count_fn, tq):
    sign = jnp.int32(-2 ** 31) if nbits == 32 else jnp.int32(0)

    def body(b, v):
        bit = lax.shift_left(jnp.int32(1), jnp.int32(nbits - 1) - b)
        cand = v | bit
        cnt = count_fn(cand ^ sign)
        return jnp.where(cnt >= target, cand, v)

    v = lax.fori_loop(0, nbits, body, jnp.zeros((tq, 1), jnp.int32))
    return v ^ sign


def _attn_kernel(qi_ref, zq_ref, zk_ref, ckv_ref, q_ref, wuk_ref, wuv_ref, o_ref,
                 key_ref, cut_ref, bias_ref, qs_ref, sc_ref, cmax_ref, m_ref, l_ref, a_ref, acc_ref, *, tq, ts, seq, n_idx,
                 n_heads, d_lat, head_dim, k_sel, idx_scale, sm_scale, slab, grp):
    i = pl.program_id(1)
    q0 = i * tq
    nkc = (q0 + tq + ts - 1) // ts
    nbits_idx = max(1, (seq - 1).bit_length())
    row = q0 + lax.broadcasted_iota(jnp.int32, (tq, 1), 0)
    int_min = jnp.int32(-2 ** 31)

    wq = zq_ref[:, IDX_DIM:IDX_DIM + n_idx] * idx_scale

    def score_body(kc, carry):
        s0 = pl.multiple_of(kc * ts, ts)
        ki = zk_ref[pl.ds(s0, ts), 0:IDX_DIM].astype(BF16)
        col = s0 + lax.broadcasted_iota(jnp.int32, (SCORE_ROWS, ts), 1)
        for r0 in range(0, tq, SCORE_ROWS):
            acc = jnp.zeros((SCORE_ROWS, ts), F32)
            for h in range(n_idx):
                lg = lax.dot_general(qi_ref[r0:r0 + SCORE_ROWS, h * IDX_DIM:(h + 1) * IDX_DIM], ki,
                                     (((1,), (1,)), ((), ())), preferred_element_type=F32)
                acc = acc + wq[r0:r0 + SCORE_ROWS, h:h + 1] * jnp.maximum(lg, 0.0)
            bits = lax.bitcast_convert_type(acc, jnp.int32)
            key = bits ^ (lax.shift_right_arithmetic(bits, 31) & jnp.int32(0x7FFFFFFF))
            key_ref[r0:r0 + SCORE_ROWS, pl.ds(s0, ts)] = jnp.where(
                col <= row[r0:r0 + SCORE_ROWS], key, int_min)
        return carry

    lax.fori_loop(0, nkc, score_body, 0)

    kf = jnp.float32(k_sel)
    thr = _bisect_largest(32, kf, lambda v: _count_ge(key_ref, nkc, ts, tq, lambda x, kc: x >= v), tq)
    n_ge = _count_ge(key_ref, nkc, ts, tq, lambda x, kc: x >= thr)

    def rev_idx(kc):
        return (seq - 1) - (kc * ts + lax.broadcasted_iota(jnp.int32, (tq, ts), 1))

    cut_ref[...] = jnp.zeros((tq, LANES), jnp.int32)

    @pl.when(jnp.max(n_ge) > kf)
    def _():
        n_gt = _count_ge(key_ref, nkc, ts, tq, lambda x, kc: x > thr)
        need = kf - n_gt
        c = _bisect_largest(
            nbits_idx, need,
            lambda v: _count_ge(key_ref, nkc, ts, tq, lambda x, kc: (x == thr) & (rev_idx(kc) >= v)), tq)
        cut_ref[...] = jnp.broadcast_to(c, (tq, LANES))

    cut = cut_ref[:, 0:1]

    rows_all = n_heads * tq
    nlc = ts // LANES
    k2 = sm_scale * LOG2E
    m_ref[...] = jnp.full((rows_all, LANES), NEG_BIG, F32)
    l_ref[...] = jnp.zeros((rows_all, LANES), F32)
    acc_ref[...] = jnp.zeros((rows_all, d_lat), F32)
    for h in range(n_heads):
        qs_ref[h * tq:(h + 1) * tq, :] = jnp.dot(q_ref[:, h * head_dim:(h + 1) * head_dim], wuk_ref[h],
                                                 preferred_element_type=F32).astype(BF16)
    n_groups = rows_all // grp
    unroll = min(GROUP_UNROLL, n_groups)

    def chunk_body(kc, carry):
        s0 = pl.multiple_of(kc * ts, ts)
        x = key_ref[:, pl.ds(s0, ts)]
        col = s0 + lax.broadcasted_iota(jnp.int32, (tq, ts), 1)
        sel = ((x > thr) | ((x == thr) & (rev_idx(kc) >= cut))) & (col <= row)
        bias_ref[...] = jnp.where(sel, 0.0, NEG_BIG)
        ckv = ckv_ref[pl.ds(s0, ts), :]

        def qk_body(g, c2):
            g0 = pl.multiple_of(g * grp, grp)
            sc = lax.dot_general(qs_ref[pl.ds(g0, grp), :], ckv, (((1,), (1,)), ((), ())),
                                 preferred_element_type=F32)
            for s in range(grp // slab):
                r0, b0 = s * slab, (s * slab) % tq
                rows_s = pl.ds(g0 + r0, slab)
                xs = [sc[r0:r0 + slab, c * LANES:(c + 1) * LANES] * k2
                      + bias_ref[b0:b0 + slab, c * LANES:(c + 1) * LANES] for c in range(nlc)]
                for c in range(nlc):
                    sc_ref[rows_s, c * LANES:(c + 1) * LANES] = xs[c]
                cmax_ref[rows_s, :] = functools.reduce(jnp.maximum, xs)
            return c2

        lax.fori_loop(0, n_groups, qk_body, 0, unroll=unroll)

        m_old = m_ref[...]
        m_new = jnp.maximum(m_old, jnp.max(cmax_ref[...], axis=1, keepdims=True))
        a_ref[...] = jnp.exp2(m_old - m_new)
        m_ref[...] = m_new

        def pv_body(g, c2):
            g0 = pl.multiple_of(g * grp, grp)
            probs = []
            for s in range(grp // slab):
                rows_s = pl.ds(g0 + s * slab, slab)
                m = m_ref[rows_s, :]
                ps = [jnp.exp2(sc_ref[rows_s, c * LANES:(c + 1) * LANES] - m) for c in range(nlc)]
                l_ref[rows_s, :] = a_ref[rows_s, :] * l_ref[rows_s, :] + functools.reduce(lambda a, b: a + b, ps)
                probs.append(jnp.concatenate([pc.astype(BF16) for pc in ps], axis=1))
            pv = jnp.dot(jnp.concatenate(probs, axis=0), ckv, preferred_element_type=F32)
            a = a_ref[pl.ds(g0, grp), :]
            acc_ref[pl.ds(g0, grp), :] = (acc_ref[pl.ds(g0, grp), :] * jnp.concatenate([a] * (d_lat // LANES), axis=1)
                                          + pv)
            return c2

        lax.fori_loop(0, n_groups, pv_body, 0, unroll=unroll)
        return carry

    lax.fori_loop(0, nkc, chunk_body, 0)

    for h in range(n_heads):
        l = jnp.sum(l_ref[h * tq:(h + 1) * tq, :], axis=1, keepdims=True)
        o_lat = (acc_ref[h * tq:(h + 1) * tq, :] / l).astype(BF16)
        o_ref[:, h * head_dim:(h + 1) * head_dim] = jnp.dot(
            o_lat, wuv_ref[h], preferred_element_type=F32).astype(o_ref.dtype)


def _attention(zh, zs, ckv, w_uk, w_uv, bsz, seq, qi_col, q_col, n_idx, kw_col, tq, ts, slab, grp):
    t = zh.shape[0]
    nh, dl, hd = w_uv.shape
    nq = seq // tq
    k_sel = min(TOPK_MAX, seq // 4)
    assert ts >= k_sel and seq % ts == 0 and seq % tq == 0 and ts % tq == 0
    assert grp % tq == 0 and (nh * tq) % grp == 0 and tq % slab == 0 and slab % BF16_ROWS == 0
    assert tq % SCORE_ROWS == 0
    kern = functools.partial(
        _attn_kernel, tq=tq, ts=ts, seq=seq, n_idx=n_idx, n_heads=nh, d_lat=dl, head_dim=hd, k_sel=k_sel,
        idx_scale=float(IDX_DIM ** -0.5 * n_idx ** -0.5), sm_scale=float(hd ** -0.5), slab=slab, grp=grp)
    return pl.pallas_call(
        kern, grid=(bsz, nq),
        in_specs=[pl.BlockSpec((tq, n_idx * IDX_DIM), lambda b, i: (b * nq + i, qi_col)),
                  pl.BlockSpec((tq, LANES), lambda b, i: (b * nq + i, kw_col)),
                  pl.BlockSpec((seq, LANES), lambda b, i: (b, kw_col)),
                  pl.BlockSpec((seq, dl), lambda b, i: (b, 0)),
                  pl.BlockSpec((tq, nh * hd), lambda b, i: (b * nq + i, q_col)),
                  pl.BlockSpec((nh, hd, dl), lambda b, i: (0, 0, 0)),
                  pl.BlockSpec((nh, dl, hd), lambda b, i: (0, 0, 0))],
        out_specs=pl.BlockSpec((tq, nh * hd), lambda b, i: (b * nq + i, 0)),
        out_shape=jax.ShapeDtypeStruct((t, nh * hd), BF16),
        scratch_shapes=[pltpu.VMEM((tq, seq), jnp.int32),
                        pltpu.VMEM((tq, LANES), jnp.int32),
                        pltpu.VMEM((tq, ts), F32),
                        pltpu.VMEM((nh * tq, dl), BF16),
                        pltpu.VMEM((nh * tq, ts), F32),
                        pltpu.VMEM((nh * tq, LANES), F32),
                        pltpu.VMEM((nh * tq, LANES), F32),
                        pltpu.VMEM((nh * tq, LANES), F32),
                        pltpu.VMEM((nh * tq, LANES), F32),
                        pltpu.VMEM((nh * tq, dl), F32)],
        compiler_params=_params(2), name="dsa_attention")(zh, zs, zs, ckv, zh, w_uk, w_uv)


def _merge_kernel(ya_ref, yb_ref, yc_ref, wa_ref, wb_ref, wc_ref, g0_ref, g1_ref, g2_ref, b_ref, o_ref):
    b = b_ref[...]
    tm = o_ref.shape[0]
    sub = min(tm, MERGE_SUB_ROWS)
    for r0 in range(0, tm, sub):
        out = None
        for k, (y_ref, w_ref, g_ref) in enumerate(((ya_ref, wa_ref, g0_ref), (yb_ref, wb_ref, g1_ref),
                                                    (yc_ref, wc_ref, g2_ref))):
            gate = jax.nn.sigmoid(g_ref[r0:r0 + sub, :].astype(F32) + b[k:k + 1, :])
            term = gate * jnp.dot(y_ref[r0:r0 + sub, :], w_ref[...], preferred_element_type=F32)
            out = term if out is None else out + term
        o_ref[r0:r0 + sub, :] = out.astype(o_ref.dtype)


def _merge(ya, yb, yc, wa, wb, wc, layer, zg, b_gate, tm, tn):
    t = ya.shape[0]
    d = wa.shape[2]
    nj = d // tn

    def yspec(y):
        return pl.BlockSpec((tm, y.shape[1]), lambda i, j: (i, 0))

    def wspec(w):
        return pl.BlockSpec((None, w.shape[1], tn), lambda i, j: (layer, 0, j))

    def gspec(k):
        return pl.BlockSpec((tm, tn), lambda i, j: (i, k * nj + j))

    return pl.pallas_call(
        _merge_kernel, grid=(t // tm, nj),
        in_specs=[yspec(ya), yspec(yb), yspec(yc), wspec(wa), wspec(wb), wspec(wc),
                  gspec(0), gspec(1), gspec(2), pl.BlockSpec((b_gate.shape[0], tn), lambda i, j: (0, j))],
        out_specs=pl.BlockSpec((tm, tn), lambda i, j: (i, j)),
        out_shape=jax.ShapeDtypeStruct((t, d), BF16),
        compiler_params=_params(2), name="merge")(ya, yb, yc, wa, wb, wc, zg, zg, zg, b_gate)


def _ln_kernel(y_ref, g_ref, b_ref, o_ref, ob_ref):
    y = y_ref[...]
    mu = jnp.mean(y, axis=-1, keepdims=True)
    yc = y - mu
    var = jnp.mean(yc * yc, axis=-1, keepdims=True)
    out = yc * lax.rsqrt(var + LN_EPS) * g_ref[...] + b_ref[...]
    o_ref[...] = out
    ob_ref[...] = out.astype(BF16)


def _layer_norm(y, g, b, rows):
    t, d = y.shape
    row_spec = pl.BlockSpec((rows, d), lambda i: (i, 0))
    vec_spec = pl.BlockSpec((1, d), lambda i: (0, 0))
    return pl.pallas_call(
        _ln_kernel, grid=(t // rows,), in_specs=[row_spec, vec_spec, vec_spec],
        out_specs=[row_spec, row_spec],
        out_shape=[jax.ShapeDtypeStruct((t, d), F32), jax.ShapeDtypeStruct((t, d), BF16)],
        compiler_params=_params(1), name="layer_norm")(y, g, b)


def _ffn_up_kernel(x_ref, wg_ref, wu_ref, cw_ref, o_ref, buf_ref, carry_ref, *, tm, sub, tiles_per_seq):
    i = pl.program_id(0)
    j = pl.program_id(1)
    seq_start = i % tiles_per_seq == 0

    @pl.when(seq_start)
    def _():
        buf_ref[0:SUBLANES, :] = jnp.zeros((SUBLANES, buf_ref.shape[1]), F32)

    @pl.when(jnp.logical_not(seq_start))
    def _():
        buf_ref[0:SUBLANES, :] = carry_ref[j]

    w = cw_ref[...]
    for r0 in range(0, tm, sub):
        xs = x_ref[r0:r0 + sub, :]
        gt = jnp.dot(xs, wg_ref[...], preferred_element_type=F32)
        up = jnp.dot(xs, wu_ref[...], preferred_element_type=F32)
        buf_ref[SUBLANES + r0:SUBLANES + r0 + sub, :] = gt
        if r0 + sub == tm:
            carry_ref[j] = gt[sub - SUBLANES:sub, :]
        s1 = buf_ref[SUBLANES - 1 + r0:SUBLANES - 1 + r0 + sub, :]
        s2 = buf_ref[SUBLANES - 2 + r0:SUBLANES - 2 + r0 + sub, :]
        conv = w[0:1, :] * s2 + w[1:2, :] * s1 + w[2:3, :] * gt
        o_ref[r0:r0 + sub, :] = (jax.nn.silu(conv) * up).astype(o_ref.dtype)


def _ffn_up(xb, w_up, layer, conv_w, seq, tm, tn):
    t, d = xb.shape
    dff = conv_w.shape[1]
    assert seq % tm == 0 and dff % tn == 0 and w_up.shape[2] == 2 * dff
    nj = dff // tn
    return pl.pallas_call(
        functools.partial(_ffn_up_kernel, tm=tm, sub=min(tm, FFN_SUB_ROWS), tiles_per_seq=seq // tm),
        grid=(t // tm, nj),
        in_specs=[pl.BlockSpec((tm, d), lambda i, j: (i, 0)),
                  pl.BlockSpec((None, d, tn), lambda i, j: (layer, 0, j)),
                  pl.BlockSpec((None, d, tn), lambda i, j: (layer, 0, nj + j)),
                  pl.BlockSpec((CONV_W, tn), lambda i, j: (0, j))],
        out_specs=pl.BlockSpec((tm, tn), lambda i, j: (i, j)),
        out_shape=jax.ShapeDtypeStruct((t, dff), BF16),
        scratch_shapes=[pltpu.VMEM((SUBLANES + tm, tn), F32), pltpu.VMEM((nj, SUBLANES, tn), F32)],
        compiler_params=_params(2), name="ffn_up")(xb, w_up, w_up, conv_w)


def _tiles(seq, d_model):
    big = d_model >= 4096
    return dict(
        mm=(1024, 1024) if big else (256, 512),
        down=(512, 512) if big else (256, 512),
        ffn_up=(2048, 256) if big else (256, 256),
        merge=(1024, 512) if big else (256, 512),
        rows=512 if big else 256,
        ln_rows=256,
        tq=128, ts=512, slab=(32, 512))


def kernel(x, w_in, b_gate, conv_a, kv_norm, w_uk, w_uv, pool_w, pool_scale, w_br_a, w_br_b, w_br_c,
           w_o, ln1_g, ln1_b, w_up, conv_ffn_w, w_down, ln2_g, ln2_b):
    bsz, seq, d = x.shape
    depth = w_in.shape[0]
    alpha = (2.0 * depth) ** 0.25
    d_conv = conv_a.shape[2]
    n_heads, head_dim, d_lat = w_uk.shape[1:]
    d_att = n_heads * head_dim
    d_pool = pool_scale.shape[1]
    d_ff = conv_ffn_w.shape[2]
    n_in = w_in.shape[2]
    n_idx = (n_in - (3 * d_conv + d_att + d_lat + IDX_DIM + d_pool + b_gate.shape[1] * d)) // (IDX_DIM + 1)
    t = bsz * seq
    tl = _tiles(seq, d)

    o_bg, o_cg, o_v = 0, d_conv, 2 * d_conv
    o_q = 3 * d_conv
    o_ckv = o_q + d_att
    o_qi = o_ckv + d_lat
    o_ki = o_qi + n_idx * IDX_DIM
    o_wi = o_ki + IDX_DIM
    o_up = o_wi + n_idx
    o_g = o_up + d_pool
    assert o_g + b_gate.shape[1] * d == n_in
    assert IDX_DIM + n_idx <= LANES

    h_qi, h_q, h_a, h_up = 0, n_idx * IDX_DIM, n_idx * IDX_DIM + d_att, n_idx * IDX_DIM + d_att + 3 * d_conv
    n_h = h_up + d_pool
    assert h_q % d_att == 0 and h_a % d_conv == 0 and h_up % d_pool == 0 and h_qi % (n_idx * IDX_DIM) == 0
    n_s = d_lat + LANES

    wi = w_in.astype(BF16)
    w_h = jnp.concatenate([wi[:, :, o_qi:o_ki], wi[:, :, o_q:o_ckv], wi[:, :, o_bg:o_q], wi[:, :, o_up:o_g]], axis=2)
    w_s = jnp.concatenate([wi[:, :, o_ckv:o_qi], wi[:, :, o_ki:o_up],
                           jnp.zeros((depth, d, LANES - IDX_DIM - n_idx), BF16)], axis=2)
    w_g = wi[:, :, o_g:]
    wa_b, wb_b, wc_b = w_br_a.astype(BF16), w_br_b.astype(BF16), w_br_c.astype(BF16)
    wo_b, wup_b, wdn_b = w_o.astype(BF16), w_up.astype(BF16), w_down.astype(BF16)

    xf = x.reshape(t, d)
    xb = xf.astype(BF16)
    for l in range(depth):
        zh = _matmul(xb, w_h, l, BF16, *tl["mm"], name="proj_wide")
        zs = _matmul(xb, w_s, l, F32, tl["mm"][0], n_s, name="proj_small")
        zg = _matmul(xb, w_g, l, BF16, *tl["mm"], name="proj_gates")

        ya = _conv_a(zh, conv_a[l], bsz, seq, h_a // d_conv, d_conv, tl["rows"])
        yc = _pool(zh, pool_w[l].astype(BF16), pool_scale[l].reshape(1, d_pool), bsz, seq, h_up // d_pool,
                   tl["rows"])
        ckv = _ckv(zs, kv_norm[l].reshape(1, d_lat), d_lat, tl["rows"])
        yb = _attention(zh, zs, ckv, w_uk[l].astype(BF16), w_uv[l].astype(BF16), bsz, seq,
                        h_qi // (n_idx * IDX_DIM), h_q // d_att, n_idx, d_lat // LANES,
                        tl["tq"], min(tl["ts"], seq), *tl["slab"])

        merged = _merge(ya, yb, yc, wa_b, wb_b, wc_b, l, zg, b_gate[l], *tl["merge"])
        y1 = _matmul(merged, wo_b, l, F32, *tl["mm"], res=xf, alpha=alpha, name="proj_out")
        xf, xb = _layer_norm(y1, ln1_g[l].reshape(1, d), ln1_b[l].reshape(1, d), tl["ln_rows"])

        h = _ffn_up(xb, wup_b, l, conv_ffn_w[l], seq, *tl["ffn_up"])
        y2 = _matmul(h, wdn_b, l, F32, *tl["down"], res=xf, alpha=alpha, name="ffn_down")
        xf, xb = _layer_norm(y2, ln2_g[l].reshape(1, d), ln2_b[l].reshape(1, d), tl["ln_rows"])
    return xf.reshape(bsz, seq, d)
```

```python
import functools

import jax
import jax.numpy as jnp
from jax import lax
from jax.experimental import pallas as pl
from jax.experimental.pallas import tpu as pltpu

IDX_DIM = 64
TOPK_MAX = 256
POOL_WINDOWS = (2, 4, 8, 16)
CONV_W = 3
LN_EPS = 1e-5
RMS_EPS = 1e-6

LANES = 128
SUBLANES = 8
BF16_ROWS = 16
VMEM_BYTES_V7X = 64 * 1024 * 1024
VMEM_LIMIT = VMEM_BYTES_V7X - 8 * 1024 * 1024

HALO = BF16_ROWS
SCORE_ROWS = 128
REPACK_COLS = 1024
GROUP_UNROLL = 4
MERGE_SUB_ROWS = 256
FFN_SUB_ROWS = 256
NEG_BIG = -1e30
LOG2E = 1.4426950408889634
F32 = jnp.float32
BF16 = jnp.bfloat16


def _params(n_axes):
    return pltpu.CompilerParams(dimension_semantics=("arbitrary",) * n_axes,
                                vmem_limit_bytes=VMEM_LIMIT)


def _mm_kernel(x_ref, w_ref, o_ref):
    o_ref[...] = jnp.dot(x_ref[...], w_ref[...], preferred_element_type=F32).astype(o_ref.dtype)


def _mm_res_kernel(x_ref, w_ref, r_ref, o_ref, *, alpha):
    acc = jnp.dot(x_ref[...], w_ref[...], preferred_element_type=F32)
    o_ref[...] = (alpha * r_ref[...] + acc).astype(o_ref.dtype)


def _matmul(x, w, layer, out_dtype, tm, tn, res=None, alpha=None, cols=None, name="mm"):
    m, k = x.shape
    c0, n = cols if cols is not None else (0, w.shape[2])
    tm, tn = min(tm, m), min(tn, n)
    assert m % tm == 0 and n % tn == 0 and c0 % tn == 0, (m, n, tm, tn, c0)
    jb = c0 // tn
    in_specs = [pl.BlockSpec((tm, k), lambda i, j: (i, 0)),
                pl.BlockSpec((None, k, tn), lambda i, j: (layer, 0, jb + j))]
    args = [x, w]
    kern = _mm_kernel
    if res is not None:
        in_specs.append(pl.BlockSpec((tm, tn), lambda i, j: (i, j)))
        args.append(res)
        kern = functools.partial(_mm_res_kernel, alpha=alpha)
    return pl.pallas_call(
        kern, grid=(m // tm, n // tn), in_specs=in_specs,
        out_specs=pl.BlockSpec((tm, tn), lambda i, j: (i, j)),
        out_shape=jax.ShapeDtypeStruct((m, n), out_dtype),
        compiler_params=_params(2), name=name)(*args)


def _conv_a_kernel(bg_ref, cg_ref, v_ref, cgp_ref, vp_ref, w_ref, o_ref, buf_ref, *, rows):
    i = pl.program_id(1)
    u = cg_ref[...].astype(F32) * v_ref[...].astype(F32)
    up = cgp_ref[...].astype(F32) * vp_ref[...].astype(F32)
    buf_ref[0:HALO, :] = jnp.where(i > 0, up, 0.0)
    buf_ref[HALO:HALO + rows, :] = u
    s1 = buf_ref[HALO - 1:HALO - 1 + rows, :]
    s2 = buf_ref[HALO - 2:HALO - 2 + rows, :]
    w = w_ref[...]
    conv = w[0:1, :] * s2 + w[1:2, :] * s1 + w[2:3, :] * u
    o_ref[...] = (bg_ref[...].astype(F32) * conv).astype(o_ref.dtype)


def _conv_a(zh, conv_w, bsz, seq, col0, width, rows):
    t = zh.shape[0]
    nb = seq // rows
    hb = rows // HALO

    def cur(c):
        return pl.BlockSpec((rows, width), lambda b, i: (b * nb + i, col0 + c))

    def prev(c):
        return pl.BlockSpec((HALO, width), lambda b, i: (jnp.maximum((b * nb + i) * hb - 1, 0), col0 + c))

    return pl.pallas_call(
        functools.partial(_conv_a_kernel, rows=rows), grid=(bsz, nb),
        in_specs=[cur(0), cur(1), cur(2), prev(1), prev(2),
                  pl.BlockSpec((CONV_W, width), lambda b, i: (0, 0))],
        out_specs=pl.BlockSpec((rows, width), lambda b, i: (b * nb + i, 0)),
        out_shape=jax.ShapeDtypeStruct((t, width), BF16),
        scratch_shapes=[pltpu.VMEM((HALO + rows, width), F32)],
        compiler_params=_params(2), name="conv_a")(zh, zh, zh, zh, zh, conv_w)


def _pool_kernel(u_ref, up_ref, pw_ref, ps_ref, o_ref, buf_ref, *, rows, group):
    i = pl.program_id(1)
    ext = HALO + rows
    pos = i * rows + lax.broadcasted_iota(jnp.int32, (rows, 1), 0)
    buf_ref[0:HALO, :] = jnp.zeros((HALO, group), F32)
    for g, win in enumerate(POOL_WINDOWS):
        sl = slice(g * group, (g + 1) * group)
        u = u_ref[:, sl].astype(F32)
        buf_ref[HALO:2 * HALO, :] = jnp.where(i > 0, up_ref[:, sl].astype(F32), 0.0)
        buf_ref[2 * HALO:2 * HALO + rows, :] = u
        k = 1
        while k < win:
            s = buf_ref[HALO:HALO + ext, :] + buf_ref[HALO - k:HALO - k + ext, :]
            buf_ref[HALO:HALO + ext, :] = s
            k *= 2
        wsum = buf_ref[2 * HALO:2 * HALO + rows, :]
        cnt = jnp.minimum(pos + 1, win).astype(F32)
        d = wsum / cnt - u
        y = jnp.dot(d.astype(BF16), pw_ref[g], preferred_element_type=F32)
        o_ref[:, sl] = (y * ps_ref[:, sl]).astype(o_ref.dtype)


def _pool(zh, pool_w, pool_scale, bsz, seq, col0, rows):
    t = zh.shape[0]
    ngroups, group, _ = pool_w.shape
    width = ngroups * group
    nb = seq // rows
    hb = rows // HALO
    return pl.pallas_call(
        functools.partial(_pool_kernel, rows=rows, group=group), grid=(bsz, nb),
        in_specs=[pl.BlockSpec((rows, width), lambda b, i: (b * nb + i, col0)),
                  pl.BlockSpec((HALO, width), lambda b, i: (jnp.maximum((b * nb + i) * hb - 1, 0), col0)),
                  pl.BlockSpec((ngroups, group, group), lambda b, i: (0, 0, 0)),
                  pl.BlockSpec((1, width), lambda b, i: (0, 0))],
        out_specs=pl.BlockSpec((rows, width), lambda b, i: (b * nb + i, 0)),
        out_shape=jax.ShapeDtypeStruct((t, width), BF16),
        scratch_shapes=[pltpu.VMEM((2 * HALO + rows, group), F32)],
        compiler_params=_params(2), name="pool")(zh, zh, pool_w, pool_scale)


def _ckv_kernel(z_ref, g_ref, o_ref, *, d_lat):
    c = z_ref[:, 0:d_lat]
    ms = jnp.mean(c * c, axis=-1, keepdims=True)
    o_ref[...] = (c * lax.rsqrt(ms + RMS_EPS) * g_ref[...]).astype(o_ref.dtype)


def _ckv(zs, kv_norm, d_lat, rows):
    t, ns = zs.shape
    return pl.pallas_call(
        functools.partial(_ckv_kernel, d_lat=d_lat), grid=(t // rows,),
        in_specs=[pl.BlockSpec((rows, ns), lambda i: (i, 0)),
                  pl.BlockSpec((1, d_lat), lambda i: (0, 0))],
        out_specs=pl.BlockSpec((rows, d_lat), lambda i: (i, 0)),
        out_shape=jax.ShapeDtypeStruct((t, d_lat), BF16),
        compiler_params=_params(1), name="ckv_norm")(zs, kv_norm)


def _count_ge(key_ref, nkc, ts, tq, pred):
    def body(kc, part):
        m = pred(key_ref[:, pl.ds(pl.multiple_of(kc * ts, ts), ts)], kc).astype(F32)
        for c in range(ts // LANES):
            part = part + m[:, c * LANES:(c + 1) * LANES]
        return part

    part = lax.fori_loop(0, nkc, body, jnp.zeros((tq, LANES), F32))
    return jnp.sum(part, axis=1, keepdims=True)


def _bisect_largest(nbits, target, count_fn, tq):
    sign = jnp.int32(-2 ** 31) if nbits == 32 else jnp.int32(0)

    def body(b, v):
        bit = lax.shift_left(jnp.int32(1), jnp.int32(nbits - 1) - b)
        cand = v | bit
        cnt = count_fn(cand ^ sign)
        return jnp.where(cnt >= target, cand, v)

    v = lax.fori_loop(0, nbits, body, jnp.zeros((tq, 1), jnp.int32))
    return v ^ sign


def _attn_kernel(qi_ref, zq_ref, zk_ref, ckv_ref, q_ref, wuk_ref, wuv_ref, o_ref,
                 key_ref, cut_ref, bias_ref, qs_ref, sc_ref, cmax_ref, m_ref, l_ref, a_ref, acc_ref, *, tq, ts, seq, n_idx,
                 n_heads, d_lat, head_dim, k_sel, idx_scale, sm_scale, slab, grp):
    i = pl.program_id(1)
    q0 = i * tq
    nkc = (q0 + tq + ts - 1) // ts
    nbits_idx = max(1, (seq - 1).bit_length())
    row = q0 + lax.broadcasted_iota(jnp.int32, (tq, 1), 0)
    int_min = jnp.int32(-2 ** 31)

    wq = zq_ref[:, IDX_DIM:IDX_DIM + n_idx] * idx_scale

    def score_body(kc, carry):
        s0 = pl.multiple_of(kc * ts, ts)
        ki = zk_ref[pl.ds(s0, ts), 0:IDX_DIM].astype(BF16)
        col = s0 + lax.broadcasted_iota(jnp.int32, (SCORE_ROWS, ts), 1)
        for r0 in range(0, tq, SCORE_ROWS):
            acc = jnp.zeros((SCORE_ROWS, ts), F32)
            for h in range(n_idx):
                lg = lax.dot_general(qi_ref[r0:r0 + SCORE_ROWS, h * IDX_DIM:(h + 1) * IDX_DIM], ki,
                                     (((1,), (1,)), ((), ())), preferred_element_type=F32)
                acc = acc + wq[r0:r0 + SCORE_ROWS, h:h + 1] * jnp.maximum(lg, 0.0)
            bits = lax.bitcast_convert_type(acc, jnp.int32)
            key = bits ^ (lax.shift_right_arithmetic(bits, 31) & jnp.int32(0x7FFFFFFF))
            key_ref[r0:r0 + SCORE_ROWS, pl.ds(s0, ts)] = jnp.where(
                col <= row[r0:r0 + SCORE_ROWS], key, int_min)
        return carry

    lax.fori_loop(0, nkc, score_body, 0)

    kf = jnp.float32(k_sel)
    thr = _bisect_largest(32, kf, lambda v: _count_ge(key_ref, nkc, ts, tq, lambda x, kc: x >= v), tq)
    n_ge = _count_ge(key_ref, nkc, ts, tq, lambda x, kc: x >= thr)

    def rev_idx(kc):
        return (seq - 1) - (kc * ts + lax.broadcasted_iota(jnp.int32, (tq, ts), 1))

    cut_ref[...] = jnp.zeros((tq, LANES), jnp.int32)

    @pl.when(jnp.max(n_ge) > kf)
    def _():
        n_gt = _count_ge(key_ref, nkc, ts, tq, lambda x, kc: x > thr)
        need = kf - n_gt
        c = _bisect_largest(
            nbits_idx, need,
            lambda v: _count_ge(key_ref, nkc, ts, tq, lambda x, kc: (x == thr) & (rev_idx(kc) >= v)), tq)
        cut_ref[...] = jnp.broadcast_to(c, (tq, LANES))

    cut = cut_ref[:, 0:1]

    rows_all = n_heads * tq
    nlc = ts // LANES
    k2 = sm_scale * LOG2E
    m_ref[...] = jnp.full((rows_all, LANES), NEG_BIG, F32)
    l_ref[...] = jnp.zeros((rows_all, LANES), F32)
    acc_ref[...] = jnp.zeros((rows_all, d_lat), F32)
    for h in range(n_heads):
        qs_ref[h * tq:(h + 1) * tq, :] = jnp.dot(q_ref[:, h * head_dim:(h + 1) * head_dim], wuk_ref[h],
                                                 preferred_element_type=F32).astype(BF16)
    n_groups = rows_all // grp
    unroll = min(GROUP_UNROLL, n_groups)

    def chunk_body(kc, carry):
        s0 = pl.multiple_of(kc * ts, ts)
        x = key_ref[:, pl.ds(s0, ts)]
        col = s0 + lax.broadcasted_iota(jnp.int32, (tq, ts), 1)
        sel = ((x > thr) | ((x == thr) & (rev_idx(kc) >= cut))) & (col <= row)
        bias_ref[...] = jnp.where(sel, 0.0, NEG_BIG)
        ckv = ckv_ref[pl.ds(s0, ts), :]

        def qk_body(g, c2):
            g0 = pl.multiple_of(g * grp, grp)
            sc = lax.dot_general(qs_ref[pl.ds(g0, grp), :], ckv, (((1,), (1,)), ((), ())),
                                 preferred_element_type=F32)
            for s in range(grp // slab):
                r0, b0 = s * slab, (s * slab) % tq
                rows_s = pl.ds(g0 + r0, slab)
                xs = [sc[r0:r0 + slab, c * LANES:(c + 1) * LANES] * k2
                      + bias_ref[b0:b0 + slab, c * LANES:(c + 1) * LANES] for c in range(nlc)]
                for c in range(nlc):
                    sc_ref[rows_s, c * LANES:(c + 1) * LANES] = xs[c]
                cmax_ref[rows_s, :] = functools.reduce(jnp.maximum, xs)
            return c2

        lax.fori_loop(0, n_groups, qk_body, 0, unroll=unroll)

        m_old = m_ref[...]
        m_new = jnp.maximum(m_old, jnp.max(cmax_ref[...], axis=1, keepdims=True))
        a_ref[...] = jnp.exp2(m_old - m_new)
        m_ref[...] = m_new

        def pv_body(g, c2):
            g0 = pl.multiple_of(g * grp, grp)
            probs = []
            for s in range(grp // slab):
                rows_s = pl.ds(g0 + s * slab, slab)
                m = m_ref[rows_s, :]
                ps = [jnp.exp2(sc_ref[rows_s, c * LANES:(c + 1) * LANES] - m) for c in range(nlc)]
                l_ref[rows_s, :] = a_ref[rows_s, :] * l_ref[rows_s, :] + functools.reduce(lambda a, b: a + b, ps)
                probs.append(jnp.concatenate([pc.astype(BF16) for pc in ps], axis=1))
            pv = jnp.dot(jnp.concatenate(probs, axis=0), ckv, preferred_element_type=F32)
            a = a_ref[pl.ds(g0, grp), :]
            acc_ref[pl.ds(g0, grp), :] = (acc_ref[pl.ds(g0, grp), :] * jnp.concatenate([a] * (d_lat // LANES), axis=1)
                                          + pv)
            return c2

        lax.fori_loop(0, n_groups, pv_body, 0, unroll=unroll)
        return carry

    lax.fori_loop(0, nkc, chunk_body, 0)

    for h in range(n_heads):
        l = jnp.sum(l_ref[h * tq:(h + 1) * tq, :], axis=1, keepdims=True)
        o_lat = (acc_ref[h * tq:(h + 1) * tq, :] / l).astype(BF16)
        o_ref[:, h * head_dim:(h + 1) * head_dim] = jnp.dot(
            o_lat, wuv_ref[h], preferred_element_type=F32).astype(o_ref.dtype)


def _attention(zh, zs, ckv, w_uk, w_uv, bsz, seq, qi_col, q_col, n_idx, kw_col, tq, ts, slab, grp):
    t = zh.shape[0]
    nh, dl, hd = w_uv.shape
    nq = seq // tq
    k_sel = min(TOPK_MAX, seq // 4)
    assert ts >= k_sel and seq % ts == 0 and seq % tq == 0 and ts % tq == 0
    assert grp % tq == 0 and (nh * tq) % grp == 0 and tq % slab == 0 and slab % BF16_ROWS == 0
    assert tq % SCORE_ROWS == 0
    kern = functools.partial(
        _attn_kernel, tq=tq, ts=ts, seq=seq, n_idx=n_idx, n_heads=nh, d_lat=dl, head_dim=hd, k_sel=k_sel,
        idx_scale=float(IDX_DIM ** -0.5 * n_idx ** -0.5), sm_scale=float(hd ** -0.5), slab=slab, grp=grp)
    return pl.pallas_call(
        kern, grid=(bsz, nq),
        in_specs=[pl.BlockSpec((tq, n_idx * IDX_DIM), lambda b, i: (b * nq + i, qi_col)),
                  pl.BlockSpec((tq, LANES), lambda b, i: (b * nq + i, kw_col)),
                  pl.BlockSpec((seq, LANES), lambda b, i: (b, kw_col)),
                  pl.BlockSpec((seq, dl), lambda b, i: (b, 0)),
                  pl.BlockSpec((tq, nh * hd), lambda b, i: (b * nq + i, q_col)),
                  pl.BlockSpec((nh, hd, dl), lambda b, i: (0, 0, 0)),
                  pl.BlockSpec((nh, dl, hd), lambda b, i: (0, 0, 0))],
        out_specs=pl.BlockSpec((tq, nh * hd), lambda b, i: (b * nq + i, 0)),
        out_shape=jax.ShapeDtypeStruct((t, nh * hd), BF16),
        scratch_shapes=[pltpu.VMEM((tq, seq), jnp.int32),
                        pltpu.VMEM((tq, LANES), jnp.int32),
                        pltpu.VMEM((tq, ts), F32),
                        pltpu.VMEM((nh * tq, dl), BF16),
                        pltpu.VMEM((nh * tq, ts), F32),
                        pltpu.VMEM((nh * tq, LANES), F32),
                        pltpu.VMEM((nh * tq, LANES), F32),
                        pltpu.VMEM((nh * tq, LANES), F32),
                        pltpu.VMEM((nh * tq, LANES), F32),
                        pltpu.VMEM((nh * tq, dl), F32)],
        compiler_params=_params(2), name="dsa_attention")(zh, zs, zs, ckv, zh, w_uk, w_uv)


def _merge_kernel(ya_ref, yb_ref, yc_ref, wa_ref, wb_ref, wc_ref, g0_ref, g1_ref, g2_ref, b_ref, o_ref):
    b = b_ref[...]
    tm = o_ref.shape[0]
    sub = min(tm, MERGE_SUB_ROWS)
    for r0 in range(0, tm, sub):
        out = None
        for k, (y_ref, w_ref, g_ref) in enumerate(((ya_ref, wa_ref, g0_ref), (yb_ref, wb_ref, g1_ref),
                                                    (yc_ref, wc_ref, g2_ref))):
            gate = jax.nn.sigmoid(g_ref[r0:r0 + sub, :].astype(F32) + b[k:k + 1, :])
            term = gate * jnp.dot(y_ref[r0:r0 + sub, :], w_ref[...], preferred_element_type=F32)
            out = term if out is None else out + term
        o_ref[r0:r0 + sub, :] = out.astype(o_ref.dtype)


def _merge(ya, yb, yc, wa, wb, wc, layer, zg, b_gate, tm, tn):
    t = ya.shape[0]
    d = wa.shape[2]
    nj = d // tn

    def yspec(y):
        return pl.BlockSpec((tm, y.shape[1]), lambda i, j: (i, 0))

    def wspec(w):
        return pl.BlockSpec((None, w.shape[1], tn), lambda i, j: (layer, 0, j))

    def gspec(k):
        return pl.BlockSpec((tm, tn), lambda i, j: (i, k * nj + j))

    return pl.pallas_call(
        _merge_kernel, grid=(t // tm, nj),
        in_specs=[yspec(ya), yspec(yb), yspec(yc), wspec(wa), wspec(wb), wspec(wc),
                  gspec(0), gspec(1), gspec(2), pl.BlockSpec((b_gate.shape[0], tn), lambda i, j: (0, j))],
        out_specs=pl.BlockSpec((tm, tn), lambda i, j: (i, j)),
        out_shape=jax.ShapeDtypeStruct((t, d), BF16),
        compiler_params=_params(2), name="merge")(ya, yb, yc, wa, wb, wc, zg, zg, zg, b_gate)


def _ln_kernel(y_ref, g_ref, b_ref, o_ref, ob_ref):
    y = y_ref[...]
    mu = jnp.mean(y, axis=-1, keepdims=True)
    yc = y - mu
    var = jnp.mean(yc * yc, axis=-1, keepdims=True)
    out = yc * lax.rsqrt(var + LN_EPS) * g_ref[...] + b_ref[...]
    o_ref[...] = out
    ob_ref[...] = out.astype(BF16)


def _layer_norm(y, g, b, rows):
    t, d = y.shape
    row_spec = pl.BlockSpec((rows, d), lambda i: (i, 0))
    vec_spec = pl.BlockSpec((1, d), lambda i: (0, 0))
    return pl.pallas_call(
        _ln_kernel, grid=(t // rows,), in_specs=[row_spec, vec_spec, vec_spec],
        out_specs=[row_spec, row_spec],
        out_shape=[jax.ShapeDtypeStruct((t, d), F32), jax.ShapeDtypeStruct((t, d), BF16)],
        compiler_params=_params(1), name="layer_norm")(y, g, b)


def _ffn_up_kernel(x_ref, wg_ref, wu_ref, cw_ref, o_ref, buf_ref, carry_ref, *, tm, sub, tiles_per_seq):
    i = pl.program_id(0)
    j = pl.program_id(1)
    seq_start = i % tiles_per_seq == 0

    @pl.when(seq_start)
    def _():
        buf_ref[0:SUBLANES, :] = jnp.zeros((SUBLANES, buf_ref.shape[1]), F32)

    @pl.when(jnp.logical_not(seq_start))
    def _():
        buf_ref[0:SUBLANES, :] = carry_ref[j]

    w = cw_ref[...]
    for r0 in range(0, tm, sub):
        xs = x_ref[r0:r0 + sub, :]
        gt = jnp.dot(xs, wg_ref[...], preferred_element_type=F32)
        up = jnp.dot(xs, wu_ref[...], preferred_element_type=F32)
        buf_ref[SUBLANES + r0:SUBLANES + r0 + sub, :] = gt
        if r0 + sub == tm:
            carry_ref[j] = gt[sub - SUBLANES:sub, :]
        s1 = buf_ref[SUBLANES - 1 + r0:SUBLANES - 1 + r0 + sub, :]
        s2 = buf_ref[SUBLANES - 2 + r0:SUBLANES - 2 + r0 + sub, :]
        conv = w[0:1, :] * s2 + w[1:2, :] * s1 + w[2:3, :] * gt
        o_ref[r0:r0 + sub, :] = (jax.nn.silu(conv) * up).astype(o_ref.dtype)


def _ffn_up(xb, w_up, layer, conv_w, seq, tm, tn):
    t, d = xb.shape
    dff = conv_w.shape[1]
    assert seq % tm == 0 and dff % tn == 0 and w_up.shape[2] == 2 * dff
    nj = dff // tn
    return pl.pallas_call(
        functools.partial(_ffn_up_kernel, tm=tm, sub=min(tm, FFN_SUB_ROWS), tiles_per_seq=seq // tm),
        grid=(t // tm, nj),
        in_specs=[pl.BlockSpec((tm, d), lambda i, j: (i, 0)),
                  pl.BlockSpec((None, d, tn), lambda i, j: (layer, 0, j)),
                  pl.BlockSpec((None, d, tn), lambda i, j: (layer, 0, nj + j)),
                  pl.BlockSpec((CONV_W, tn), lambda i, j: (0, j))],
        out_specs=pl.BlockSpec((tm, tn), lambda i, j: (i, j)),
        out_shape=jax.ShapeDtypeStruct((t, dff), BF16),
        scratch_shapes=[pltpu.VMEM((SUBLANES + tm, tn), F32), pltpu.VMEM((nj, SUBLANES, tn), F32)],
        compiler_params=_params(2), name="ffn_up")(xb, w_up, w_up, conv_w)


def _repack_kernel(w_ref, hg_ref, s_ref, *, segs_hg, segs_s):
    def copy_segments(dst_ref, segs):
        dst = 0
        for src, width in segs:
            for c in range(0, width, REPACK_COLS):
                n = min(REPACK_COLS, width - c)
                dst_ref[:, dst + c:dst + c + n] = w_ref[:, src + c:src + c + n].astype(dst_ref.dtype)
            dst += width
        return dst

    copy_segments(hg_ref, segs_hg)
    end = copy_segments(s_ref, segs_s)
    s_ref[:, end:] = jnp.zeros((s_ref.shape[0], s_ref.shape[1] - end), s_ref.dtype)


def _repack_w_in(w_in, segs_hg, segs_s, n_s, rows):
    depth, d, n_in = w_in.shape
    n_hg = sum(w for _, w in segs_hg)
    return pl.pallas_call(
        functools.partial(_repack_kernel, segs_hg=segs_hg, segs_s=segs_s), grid=(depth, d // rows),
        in_specs=[pl.BlockSpec((None, rows, n_in), lambda l, r: (l, r, 0))],
        out_specs=[pl.BlockSpec((None, rows, n_hg), lambda l, r: (l, r, 0)),
                   pl.BlockSpec((None, rows, n_s), lambda l, r: (l, r, 0))],
        out_shape=[jax.ShapeDtypeStruct((depth, d, n_hg), BF16), jax.ShapeDtypeStruct((depth, d, n_s), BF16)],
        compiler_params=_params(2), name="repack_w_in")(w_in)


def _tiles(seq, d_model):
    big = d_model >= 4096
    return dict(
        mm=(1024, 1024) if big else (256, 512),
        down=(512, 512) if big else (256, 512),
        ffn_up=(2048, 256) if big else (256, 256),
        merge=(1024, 512) if big else (256, 512),
        rows=512 if big else 256,
        ln_rows=256, repack_rows=64,
        tq=128, ts=512, slab=(32, 512))


def kernel(x, w_in, b_gate, conv_a, kv_norm, w_uk, w_uv, pool_w, pool_scale, w_br_a, w_br_b, w_br_c,
           w_o, ln1_g, ln1_b, w_up, conv_ffn_w, w_down, ln2_g, ln2_b):
    bsz, seq, d = x.shape
    depth = w_in.shape[0]
    alpha = (2.0 * depth) ** 0.25
    d_conv = conv_a.shape[2]
    n_heads, head_dim, d_lat = w_uk.shape[1:]
    d_att = n_heads * head_dim
    d_pool = pool_scale.shape[1]
    d_ff = conv_ffn_w.shape[2]
    n_in = w_in.shape[2]
    n_idx = (n_in - (3 * d_conv + d_att + d_lat + IDX_DIM + d_pool + b_gate.shape[1] * d)) // (IDX_DIM + 1)
    t = bsz * seq
    tl = _tiles(seq, d)

    o_bg, o_cg, o_v = 0, d_conv, 2 * d_conv
    o_q = 3 * d_conv
    o_ckv = o_q + d_att
    o_qi = o_ckv + d_lat
    o_ki = o_qi + n_idx * IDX_DIM
    o_wi = o_ki + IDX_DIM
    o_up = o_wi + n_idx
    o_g = o_up + d_pool
    assert o_g + b_gate.shape[1] * d == n_in
    assert IDX_DIM + n_idx <= LANES

    h_qi, h_q, h_a, h_up = 0, n_idx * IDX_DIM, n_idx * IDX_DIM + d_att, n_idx * IDX_DIM + d_att + 3 * d_conv
    n_h = h_up + d_pool
    assert h_q % d_att == 0 and h_a % d_conv == 0 and h_up % d_pool == 0 and h_qi % (n_idx * IDX_DIM) == 0
    n_s = d_lat + LANES

    segs_hg = [(o_qi, o_ki - o_qi), (o_q, o_ckv - o_q), (o_bg, o_q - o_bg), (o_up, o_g - o_up), (o_g, n_in - o_g)]
    segs_s = [(o_ckv, o_qi - o_ckv), (o_ki, o_up - o_ki)]
    w_hg, w_s = _repack_w_in(w_in, segs_hg, segs_s, n_s, tl["repack_rows"])
    wa_b, wb_b, wc_b = w_br_a.astype(BF16), w_br_b.astype(BF16), w_br_c.astype(BF16)
    wo_b, wup_b, wdn_b = w_o.astype(BF16), w_up.astype(BF16), w_down.astype(BF16)

    xf = x.reshape(t, d)
    xb = xf.astype(BF16)
    for l in range(depth):
        zh = _matmul(xb, w_hg, l, BF16, *tl["mm"], cols=(0, n_h), name="proj_wide")
        zs = _matmul(xb, w_s, l, F32, tl["mm"][0], n_s, name="proj_small")
        zg = _matmul(xb, w_hg, l, BF16, *tl["mm"], cols=(n_h, n_in - o_g), name="proj_gates")

        ya = _conv_a(zh, conv_a[l], bsz, seq, h_a // d_conv, d_conv, tl["rows"])
        yc = _pool(zh, pool_w[l].astype(BF16), pool_scale[l].reshape(1, d_pool), bsz, seq, h_up // d_pool,
                   tl["rows"])
        ckv = _ckv(zs, kv_norm[l].reshape(1, d_lat), d_lat, tl["rows"])
        yb = _attention(zh, zs, ckv, w_uk[l].astype(BF16), w_uv[l].astype(BF16), bsz, seq,
                        h_qi // (n_idx * IDX_DIM), h_q // d_att, n_idx, d_lat // LANES,
                        tl["tq"], min(tl["ts"], seq), *tl["slab"])

        merged = _merge(ya, yb, yc, wa_b, wb_b, wc_b, l, zg, b_gate[l], *tl["merge"])
        y1 = _matmul(merged, wo_b, l, F32, *tl["mm"], res=xf, alpha=alpha, name="proj_out")
        xf, xb = _layer_norm(y1, ln1_g[l].reshape(1, d), ln1_b[l].reshape(1, d), tl["ln_rows"])

        h = _ffn_up(xb, wup_b, l, conv_ffn_w[l], seq, *tl["ffn_up"])
        y2 = _matmul(h, wdn_b, l, F32, *tl["down"], res=xf, alpha=alpha, name="ffn_down")
        xf, xb = _layer_norm(y2, ln2_g[l].reshape(1, d), ln2_b[l].reshape(1, d), tl["ln_rows"])
    return xf.reshape(bsz, seq, d)
```

```python
import functools

import jax
import jax.numpy as jnp
from jax import lax
from jax.experimental import pallas as pl
from jax.experimental.pallas import tpu as pltpu

IDX_DIM = 64
TOPK_MAX = 256
POOL_WINDOWS = (2, 4, 8, 16)
CONV_W = 3
LN_EPS = 1e-5
RMS_EPS = 1e-6

LANES = 128
SUBLANES = 8
BF16_ROWS = 16
VMEM_BYTES_V7X = 64 * 1024 * 1024
VMEM_LIMIT = VMEM_BYTES_V7X - 8 * 1024 * 1024

HALO = BF16_ROWS
SCORE_ROWS = 128
COUNT_ROWS = 64
GROUP_UNROLL = 4
MERGE_SUB_ROWS = 256
FFN_SUB_ROWS = 256
NEG_BIG = -1e30
LOG2E = 1.4426950408889634
F32 = jnp.float32
BF16 = jnp.bfloat16


def _params(n_axes):
    return pltpu.CompilerParams(dimension_semantics=("arbitrary",) * n_axes,
                                vmem_limit_bytes=VMEM_LIMIT)


def _mm_kernel(x_ref, w_ref, o_ref):
    o_ref[...] = jnp.dot(x_ref[...], w_ref[...], preferred_element_type=F32).astype(o_ref.dtype)


def _mm_res_kernel(x_ref, w_ref, r_ref, o_ref, *, alpha):
    acc = jnp.dot(x_ref[...], w_ref[...], preferred_element_type=F32)
    o_ref[...] = (alpha * r_ref[...] + acc).astype(o_ref.dtype)


def _matmul(x, w, layer, out_dtype, tm, tn, res=None, alpha=None, name="mm"):
    m, k = x.shape
    n = w.shape[2]
    tm, tn = min(tm, m), min(tn, n)
    assert m % tm == 0 and n % tn == 0, (m, n, tm, tn)
    in_specs = [pl.BlockSpec((tm, k), lambda i, j: (i, 0)),
                pl.BlockSpec((None, k, tn), lambda i, j: (layer, 0, j))]
    args = [x, w]
    kern = _mm_kernel
    if res is not None:
        in_specs.append(pl.BlockSpec((tm, tn), lambda i, j: (i, j)))
        args.append(res)
        kern = functools.partial(_mm_res_kernel, alpha=alpha)
    return pl.pallas_call(
        kern, grid=(m // tm, n // tn), in_specs=in_specs,
        out_specs=pl.BlockSpec((tm, tn), lambda i, j: (i, j)),
        out_shape=jax.ShapeDtypeStruct((m, n), out_dtype),
        compiler_params=_params(2), name=name)(*args)


def _conv_a_kernel(bg_ref, cg_ref, v_ref, cgp_ref, vp_ref, w_ref, o_ref, buf_ref, *, rows):
    i = pl.program_id(1)
    u = cg_ref[...].astype(F32) * v_ref[...].astype(F32)
    up = cgp_ref[...].astype(F32) * vp_ref[...].astype(F32)
    buf_ref[0:HALO, :] = jnp.where(i > 0, up, 0.0)
    buf_ref[HALO:HALO + rows, :] = u
    s1 = buf_ref[HALO - 1:HALO - 1 + rows, :]
    s2 = buf_ref[HALO - 2:HALO - 2 + rows, :]
    w = w_ref[...]
    conv = w[0:1, :] * s2 + w[1:2, :] * s1 + w[2:3, :] * u
    o_ref[...] = (bg_ref[...].astype(F32) * conv).astype(o_ref.dtype)


def _conv_a(zh, conv_w, bsz, seq, col0, width, rows):
    t = zh.shape[0]
    nb = seq // rows
    hb = rows // HALO

    def cur(c):
        return pl.BlockSpec((rows, width), lambda b, i: (b * nb + i, col0 + c))

    def prev(c):
        return pl.BlockSpec((HALO, width), lambda b, i: (jnp.maximum((b * nb + i) * hb - 1, 0), col0 + c))

    return pl.pallas_call(
        functools.partial(_conv_a_kernel, rows=rows), grid=(bsz, nb),
        in_specs=[cur(0), cur(1), cur(2), prev(1), prev(2),
                  pl.BlockSpec((CONV_W, width), lambda b, i: (0, 0))],
        out_specs=pl.BlockSpec((rows, width), lambda b, i: (b * nb + i, 0)),
        out_shape=jax.ShapeDtypeStruct((t, width), BF16),
        scratch_shapes=[pltpu.VMEM((HALO + rows, width), F32)],
        compiler_params=_params(2), name="conv_a")(zh, zh, zh, zh, zh, conv_w)


def _pool_kernel(u_ref, up_ref, pw_ref, ps_ref, o_ref, buf_ref, *, rows, group):
    i = pl.program_id(1)
    ext = HALO + rows
    pos = i * rows + lax.broadcasted_iota(jnp.int32, (rows, 1), 0)
    buf_ref[0:HALO, :] = jnp.zeros((HALO, group), F32)
    for g, win in enumerate(POOL_WINDOWS):
        sl = slice(g * group, (g + 1) * group)
        u = u_ref[:, sl].astype(F32)
        buf_ref[HALO:2 * HALO, :] = jnp.where(i > 0, up_ref[:, sl].astype(F32), 0.0)
        buf_ref[2 * HALO:2 * HALO + rows, :] = u
        k = 1
        while k < win:
            s = buf_ref[HALO:HALO + ext, :] + buf_ref[HALO - k:HALO - k + ext, :]
            buf_ref[HALO:HALO + ext, :] = s
            k *= 2
        wsum = buf_ref[2 * HALO:2 * HALO + rows, :]
        cnt = jnp.minimum(pos + 1, win).astype(F32)
        d = wsum / cnt - u
        y = jnp.dot(d.astype(BF16), pw_ref[g], preferred_element_type=F32)
        o_ref[:, sl] = (y * ps_ref[:, sl]).astype(o_ref.dtype)


def _pool(zh, pool_w, pool_scale, bsz, seq, col0, rows):
    t = zh.shape[0]
    ngroups, group, _ = pool_w.shape
    width = ngroups * group
    nb = seq // rows
    hb = rows // HALO
    return pl.pallas_call(
        functools.partial(_pool_kernel, rows=rows, group=group), grid=(bsz, nb),
        in_specs=[pl.BlockSpec((rows, width), lambda b, i: (b * nb + i, col0)),
                  pl.BlockSpec((HALO, width), lambda b, i: (jnp.maximum((b * nb + i) * hb - 1, 0), col0)),
                  pl.BlockSpec((ngroups, group, group), lambda b, i: (0, 0, 0)),
                  pl.BlockSpec((1, width), lambda b, i: (0, 0))],
        out_specs=pl.BlockSpec((rows, width), lambda b, i: (b * nb + i, 0)),
        out_shape=jax.ShapeDtypeStruct((t, width), BF16),
        scratch_shapes=[pltpu.VMEM((2 * HALO + rows, group), F32)],
        compiler_params=_params(2), name="pool")(zh, zh, pool_w, pool_scale)


def _ckv_kernel(z_ref, g_ref, o_ref, *, d_lat):
    c = z_ref[:, 0:d_lat]
    ms = jnp.mean(c * c, axis=-1, keepdims=True)
    o_ref[...] = (c * lax.rsqrt(ms + RMS_EPS) * g_ref[...]).astype(o_ref.dtype)


def _ckv(zs, kv_norm, d_lat, rows):
    t, ns = zs.shape
    return pl.pallas_call(
        functools.partial(_ckv_kernel, d_lat=d_lat), grid=(t // rows,),
        in_specs=[pl.BlockSpec((rows, ns), lambda i: (i, 0)),
                  pl.BlockSpec((1, d_lat), lambda i: (0, 0))],
        out_specs=pl.BlockSpec((rows, d_lat), lambda i: (i, 0)),
        out_shape=jax.ShapeDtypeStruct((t, d_lat), BF16),
        compiler_params=_params(1), name="ckv_norm")(zs, kv_norm)


def _count_ge(key_ref, nkc, ts, tq, pred):
    def body(kc, part):
        m = pred(key_ref[:, pl.ds(pl.multiple_of(kc * ts, ts), ts)], kc).astype(F32)
        for c in range(ts // LANES):
            part = part + m[:, c * LANES:(c + 1) * LANES]
        return part

    part = lax.fori_loop(0, nkc, body, jnp.zeros((tq, LANES), F32))
    return jnp.sum(part, axis=1, keepdims=True)


def _kth_largest_key(keyt_ref, nkc, ts, tq, k):
    sign = jnp.int32(-2 ** 31)

    def count_ge(cand):
        def body(kc, part):
            m = (keyt_ref[pl.ds(pl.multiple_of(kc * ts, ts), ts), :] >= cand).astype(F32)
            for r in range(ts // COUNT_ROWS):
                part = part + m[r * COUNT_ROWS:(r + 1) * COUNT_ROWS, :]
            return part

        part = lax.fori_loop(0, nkc, body, jnp.zeros((COUNT_ROWS, tq), F32))
        return jnp.sum(part, axis=0, keepdims=True)

    def body(b, v):
        cand = v | lax.shift_left(jnp.int32(1), jnp.int32(31) - b)
        return jnp.where(count_ge(cand ^ sign) >= k, cand, v)

    thr = lax.fori_loop(0, 32, body, jnp.zeros((1, tq), jnp.int32)) ^ sign
    return thr, count_ge(thr)


def _bisect_largest(nbits, target, count_fn, tq):
    sign = jnp.int32(-2 ** 31) if nbits == 32 else jnp.int32(0)

    def body(b, v):
        bit = lax.shift_left(jnp.int32(1), jnp.int32(nbits - 1) - b)
        cand = v | bit
        cnt = count_fn(cand ^ sign)
        return jnp.where(cnt >= target, cand, v)

    v = lax.fori_loop(0, nbits, body, jnp.zeros((tq, 1), jnp.int32))
    return v ^ sign


def _attn_kernel(qi_ref, zq_ref, zk_ref, ckv_ref, q_ref, wuk_ref, wuv_ref, o_ref,
                 key_ref, keyt_ref, cut_ref, bias_ref, qs_ref, sc_ref, cmax_ref, m_ref, l_ref, a_ref, acc_ref, *, tq, ts,
                 seq, n_idx,
                 n_heads, d_lat, head_dim, k_sel, idx_scale, sm_scale, slab, grp):
    i = pl.program_id(1)
    q0 = i * tq
    nkc = (q0 + tq + ts - 1) // ts
    nbits_idx = max(1, (seq - 1).bit_length())
    row = q0 + lax.broadcasted_iota(jnp.int32, (tq, 1), 0)
    int_min = jnp.int32(-2 ** 31)

    wq = zq_ref[:, IDX_DIM:IDX_DIM + n_idx] * idx_scale

    def score_body(kc, carry):
        s0 = pl.multiple_of(kc * ts, ts)
        ki = zk_ref[pl.ds(s0, ts), 0:IDX_DIM].astype(BF16)
        col = s0 + lax.broadcasted_iota(jnp.int32, (SCORE_ROWS, ts), 1)
        for r0 in range(0, tq, SCORE_ROWS):
            acc = jnp.zeros((SCORE_ROWS, ts), F32)
            for h in range(n_idx):
                lg = lax.dot_general(qi_ref[r0:r0 + SCORE_ROWS, h * IDX_DIM:(h + 1) * IDX_DIM], ki,
                                     (((1,), (1,)), ((), ())), preferred_element_type=F32)
                acc = acc + wq[r0:r0 + SCORE_ROWS, h:h + 1] * jnp.maximum(lg, 0.0)
            bits = lax.bitcast_convert_type(acc, jnp.int32)
            key = bits ^ (lax.shift_right_arithmetic(bits, 31) & jnp.int32(0x7FFFFFFF))
            key = jnp.where(col <= row[r0:r0 + SCORE_ROWS], key, int_min)
            key_ref[r0:r0 + SCORE_ROWS, pl.ds(s0, ts)] = key
            keyt_ref[pl.ds(s0, ts), r0:r0 + SCORE_ROWS] = key.T
        return carry

    lax.fori_loop(0, nkc, score_body, 0)

    kf = jnp.float32(k_sel)
    thr_t, n_ge = _kth_largest_key(keyt_ref, nkc, ts, tq, kf)
    thr = jnp.broadcast_to(thr_t, (SUBLANES, tq)).T[:, 0:1]

    def rev_idx(kc):
        return (seq - 1) - (kc * ts + lax.broadcasted_iota(jnp.int32, (tq, ts), 1))

    cut_ref[...] = jnp.zeros((tq, LANES), jnp.int32)

    @pl.when(jnp.max(n_ge) > kf)
    def _():
        n_gt = _count_ge(key_ref, nkc, ts, tq, lambda x, kc: x > thr)
        need = kf - n_gt
        c = _bisect_largest(
            nbits_idx, need,
            lambda v: _count_ge(key_ref, nkc, ts, tq, lambda x, kc: (x == thr) & (rev_idx(kc) >= v)), tq)
        cut_ref[...] = jnp.broadcast_to(c, (tq, LANES))

    cut = cut_ref[:, 0:1]

    rows_all = n_heads * tq
    nlc = ts // LANES
    k2 = sm_scale * LOG2E
    m_ref[...] = jnp.full((rows_all, LANES), NEG_BIG, F32)
    l_ref[...] = jnp.zeros((rows_all, LANES), F32)
    acc_ref[...] = jnp.zeros((rows_all, d_lat), F32)
    for h in range(n_heads):
        qs_ref[h * tq:(h + 1) * tq, :] = jnp.dot(q_ref[:, h * head_dim:(h + 1) * head_dim], wuk_ref[h],
                                                 preferred_element_type=F32).astype(BF16)
    n_groups = rows_all // grp
    unroll = min(GROUP_UNROLL, n_groups)

    def chunk_body(kc, carry):
        s0 = pl.multiple_of(kc * ts, ts)
        x = key_ref[:, pl.ds(s0, ts)]
        col = s0 + lax.broadcasted_iota(jnp.int32, (tq, ts), 1)
        sel = ((x > thr) | ((x == thr) & (rev_idx(kc) >= cut))) & (col <= row)
        bias_ref[...] = jnp.where(sel, 0.0, NEG_BIG)
        ckv = ckv_ref[pl.ds(s0, ts), :]

        def qk_body(g, c2):
            g0 = pl.multiple_of(g * grp, grp)
            sc = lax.dot_general(qs_ref[pl.ds(g0, grp), :], ckv, (((1,), (1,)), ((), ())),
                                 preferred_element_type=F32)
            for s in range(grp // slab):
                r0, b0 = s * slab, (s * slab) % tq
                rows_s = pl.ds(g0 + r0, slab)
                xs = [sc[r0:r0 + slab, c * LANES:(c + 1) * LANES] * k2
                      + bias_ref[b0:b0 + slab, c * LANES:(c + 1) * LANES] for c in range(nlc)]
                for c in range(nlc):
                    sc_ref[rows_s, c * LANES:(c + 1) * LANES] = xs[c]
                cmax_ref[rows_s, :] = functools.reduce(jnp.maximum, xs)
            return c2

        lax.fori_loop(0, n_groups, qk_body, 0, unroll=unroll)

        m_old = m_ref[...]
        m_new = jnp.maximum(m_old, jnp.max(cmax_ref[...], axis=1, keepdims=True))
        a_ref[...] = jnp.exp2(m_old - m_new)
        m_ref[...] = m_new

        def pv_body(g, c2):
            g0 = pl.multiple_of(g * grp, grp)
            probs = []
            for s in range(grp // slab):
                rows_s = pl.ds(g0 + s * slab, slab)
                m = m_ref[rows_s, :]
                ps = [jnp.exp2(sc_ref[rows_s, c * LANES:(c + 1) * LANES] - m) for c in range(nlc)]
                l_ref[rows_s, :] = a_ref[rows_s, :] * l_ref[rows_s, :] + functools.reduce(lambda a, b: a + b, ps)
                probs.append(jnp.concatenate([pc.astype(BF16) for pc in ps], axis=1))
            pv = jnp.dot(jnp.concatenate(probs, axis=0), ckv, preferred_element_type=F32)
            a = a_ref[pl.ds(g0, grp), :]
            acc_ref[pl.ds(g0, grp), :] = (acc_ref[pl.ds(g0, grp), :] * jnp.concatenate([a] * (d_lat // LANES), axis=1)
                                          + pv)
            return c2

        lax.fori_loop(0, n_groups, pv_body, 0, unroll=unroll)
        return carry

    lax.fori_loop(0, nkc, chunk_body, 0)

    for h in range(n_heads):
        l = jnp.sum(l_ref[h * tq:(h + 1) * tq, :], axis=1, keepdims=True)
        o_lat = (acc_ref[h * tq:(h + 1) * tq, :] / l).astype(BF16)
        o_ref[:, h * head_dim:(h + 1) * head_dim] = jnp.dot(
            o_lat, wuv_ref[h], preferred_element_type=F32).astype(o_ref.dtype)


def _attention(zh, zs, ckv, w_uk, w_uv, bsz, seq, qi_col, q_col, n_idx, kw_col, tq, ts, slab, grp):
    t = zh.shape[0]
    nh, dl, hd = w_uv.shape
    nq = seq // tq
    k_sel = min(TOPK_MAX, seq // 4)
    assert ts >= k_sel and seq % ts == 0 and seq % tq == 0 and ts % tq == 0
    assert grp % tq == 0 and (nh * tq) % grp == 0 and tq % slab == 0 and slab % BF16_ROWS == 0
    assert tq % SCORE_ROWS == 0
    kern = functools.partial(
        _attn_kernel, tq=tq, ts=ts, seq=seq, n_idx=n_idx, n_heads=nh, d_lat=dl, head_dim=hd, k_sel=k_sel,
        idx_scale=float(IDX_DIM ** -0.5 * n_idx ** -0.5), sm_scale=float(hd ** -0.5), slab=slab, grp=grp)
    return pl.pallas_call(
        kern, grid=(bsz, nq),
        in_specs=[pl.BlockSpec((tq, n_idx * IDX_DIM), lambda b, i: (b * nq + i, qi_col)),
                  pl.BlockSpec((tq, LANES), lambda b, i: (b * nq + i, kw_col)),
                  pl.BlockSpec((seq, LANES), lambda b, i: (b, kw_col)),
                  pl.BlockSpec((seq, dl), lambda b, i: (b, 0)),
                  pl.BlockSpec((tq, nh * hd), lambda b, i: (b * nq + i, q_col)),
                  pl.BlockSpec((nh, hd, dl), lambda b, i: (0, 0, 0)),
                  pl.BlockSpec((nh, dl, hd), lambda b, i: (0, 0, 0))],
        out_specs=pl.BlockSpec((tq, nh * hd), lambda b, i: (b * nq + i, 0)),
        out_shape=jax.ShapeDtypeStruct((t, nh * hd), BF16),
        scratch_shapes=[pltpu.VMEM((tq, seq), jnp.int32),
                        pltpu.VMEM((seq, tq), jnp.int32),
                        pltpu.VMEM((tq, LANES), jnp.int32),
                        pltpu.VMEM((tq, ts), F32),
                        pltpu.VMEM((nh * tq, dl), BF16),
                        pltpu.VMEM((nh * tq, ts), F32),
                        pltpu.VMEM((nh * tq, LANES), F32),
                        pltpu.VMEM((nh * tq, LANES), F32),
                        pltpu.VMEM((nh * tq, LANES), F32),
                        pltpu.VMEM((nh * tq, LANES), F32),
                        pltpu.VMEM((nh * tq, dl), F32)],
        compiler_params=_params(2), name="dsa_attention")(zh, zs, zs, ckv, zh, w_uk, w_uv)


def _merge_kernel(ya_ref, yb_ref, yc_ref, wa_ref, wb_ref, wc_ref, g0_ref, g1_ref, g2_ref, b_ref, o_ref):
    b = b_ref[...]
    tm = o_ref.shape[0]
    sub = min(tm, MERGE_SUB_ROWS)
    for r0 in range(0, tm, sub):
        out = None
        for k, (y_ref, w_ref, g_ref) in enumerate(((ya_ref, wa_ref, g0_ref), (yb_ref, wb_ref, g1_ref),
                                                    (yc_ref, wc_ref, g2_ref))):
            gate = jax.nn.sigmoid(g_ref[r0:r0 + sub, :].astype(F32) + b[k:k + 1, :])
            term = gate * jnp.dot(y_ref[r0:r0 + sub, :], w_ref[...], preferred_element_type=F32)
            out = term if out is None else out + term
        o_ref[r0:r0 + sub, :] = out.astype(o_ref.dtype)


def _merge(ya, yb, yc, wa, wb, wc, layer, zg, b_gate, tm, tn):
    t = ya.shape[0]
    d = wa.shape[2]
    nj = d // tn

    def yspec(y):
        return pl.BlockSpec((tm, y.shape[1]), lambda i, j: (i, 0))

    def wspec(w):
        return pl.BlockSpec((None, w.shape[1], tn), lambda i, j: (layer, 0, j))

    def gspec(k):
        return pl.BlockSpec((tm, tn), lambda i, j: (i, k * nj + j))

    return pl.pallas_call(
        _merge_kernel, grid=(t // tm, nj),
        in_specs=[yspec(ya), yspec(yb), yspec(yc), wspec(wa), wspec(wb), wspec(wc),
                  gspec(0), gspec(1), gspec(2), pl.BlockSpec((b_gate.shape[0], tn), lambda i, j: (0, j))],
        out_specs=pl.BlockSpec((tm, tn), lambda i, j: (i, j)),
        out_shape=jax.ShapeDtypeStruct((t, d), BF16),
        compiler_params=_params(2), name="merge")(ya, yb, yc, wa, wb, wc, zg, zg, zg, b_gate)


def _ln_kernel(y_ref, g_ref, b_ref, o_ref, ob_ref):
    y = y_ref[...]
    mu = jnp.mean(y, axis=-1, keepdims=True)
    yc = y - mu
    var = jnp.mean(yc * yc, axis=-1, keepdims=True)
    out = yc * lax.rsqrt(var + LN_EPS) * g_ref[...] + b_ref[...]
    o_ref[...] = out
    ob_ref[...] = out.astype(BF16)


def _layer_norm(y, g, b, rows):
    t, d = y.shape
    row_spec = pl.BlockSpec((rows, d), lambda i: (i, 0))
    vec_spec = pl.BlockSpec((1, d), lambda i: (0, 0))
    return pl.pallas_call(
        _ln_kernel, grid=(t // rows,), in_specs=[row_spec, vec_spec, vec_spec],
        out_specs=[row_spec, row_spec],
        out_shape=[jax.ShapeDtypeStruct((t, d), F32), jax.ShapeDtypeStruct((t, d), BF16)],
        compiler_params=_params(1), name="layer_norm")(y, g, b)


def _ffn_up_kernel(x_ref, wg_ref, wu_ref, cw_ref, o_ref, buf_ref, carry_ref, *, tm, sub, tiles_per_seq):
    i = pl.program_id(0)
    j = pl.program_id(1)
    seq_start = i % tiles_per_seq == 0

    @pl.when(seq_start)
    def _():
        buf_ref[0:SUBLANES, :] = jnp.zeros((SUBLANES, buf_ref.shape[1]), F32)

    @pl.when(jnp.logical_not(seq_start))
    def _():
        buf_ref[0:SUBLANES, :] = carry_ref[j]

    w = cw_ref[...]
    for r0 in range(0, tm, sub):
        xs = x_ref[r0:r0 + sub, :]
        gt = jnp.dot(xs, wg_ref[...], preferred_element_type=F32)
        up = jnp.dot(xs, wu_ref[...], preferred_element_type=F32)
        buf_ref[SUBLANES + r0:SUBLANES + r0 + sub, :] = gt
        if r0 + sub == tm:
            carry_ref[j] = gt[sub - SUBLANES:sub, :]
        s1 = buf_ref[SUBLANES - 1 + r0:SUBLANES - 1 + r0 + sub, :]
        s2 = buf_ref[SUBLANES - 2 + r0:SUBLANES - 2 + r0 + sub, :]
        conv = w[0:1, :] * s2 + w[1:2, :] * s1 + w[2:3, :] * gt
        o_ref[r0:r0 + sub, :] = (jax.nn.silu(conv) * up).astype(o_ref.dtype)


def _ffn_up(xb, w_up, layer, conv_w, seq, tm, tn):
    t, d = xb.shape
    dff = conv_w.shape[1]
    assert seq % tm == 0 and dff % tn == 0 and w_up.shape[2] == 2 * dff
    nj = dff // tn
    return pl.pallas_call(
        functools.partial(_ffn_up_kernel, tm=tm, sub=min(tm, FFN_SUB_ROWS), tiles_per_seq=seq // tm),
        grid=(t // tm, nj),
        in_specs=[pl.BlockSpec((tm, d), lambda i, j: (i, 0)),
                  pl.BlockSpec((None, d, tn), lambda i, j: (layer, 0, j)),
                  pl.BlockSpec((None, d, tn), lambda i, j: (layer, 0, nj + j)),
                  pl.BlockSpec((CONV_W, tn), lambda i, j: (0, j))],
        out_specs=pl.BlockSpec((tm, tn), lambda i, j: (i, j)),
        out_shape=jax.ShapeDtypeStruct((t, dff), BF16),
        scratch_shapes=[pltpu.VMEM((SUBLANES + tm, tn), F32), pltpu.VMEM((nj, SUBLANES, tn), F32)],
        compiler_params=_params(2), name="ffn_up")(xb, w_up, w_up, conv_w)


def _tiles(seq, d_model):
    big = d_model >= 4096
    return dict(
        mm=(1024, 1024) if big else (256, 512),
        down=(512, 512) if big else (256, 512),
        ffn_up=(2048, 256) if big else (256, 256),
        merge=(1024, 512) if big else (256, 512),
        rows=512 if big else 256,
        ln_rows=256,
        tq=128, ts=512, slab=(32, 512))


def kernel(x, w_in, b_gate, conv_a, kv_norm, w_uk, w_uv, pool_w, pool_scale, w_br_a, w_br_b, w_br_c,
           w_o, ln1_g, ln1_b, w_up, conv_ffn_w, w_down, ln2_g, ln2_b):
    bsz, seq, d = x.shape
    depth = w_in.shape[0]
    alpha = (2.0 * depth) ** 0.25
    d_conv = conv_a.shape[2]
    n_heads, head_dim, d_lat = w_uk.shape[1:]
    d_att = n_heads * head_dim
    d_pool = pool_scale.shape[1]
    d_ff = conv_ffn_w.shape[2]
    n_in = w_in.shape[2]
    n_idx = (n_in - (3 * d_conv + d_att + d_lat + IDX_DIM + d_pool + b_gate.shape[1] * d)) // (IDX_DIM + 1)
    t = bsz * seq
    tl = _tiles(seq, d)

    o_bg, o_cg, o_v = 0, d_conv, 2 * d_conv
    o_q = 3 * d_conv
    o_ckv = o_q + d_att
    o_qi = o_ckv + d_lat
    o_ki = o_qi + n_idx * IDX_DIM
    o_wi = o_ki + IDX_DIM
    o_up = o_wi + n_idx
    o_g = o_up + d_pool
    assert o_g + b_gate.shape[1] * d == n_in
    assert IDX_DIM + n_idx <= LANES

    h_qi, h_q, h_a, h_up = 0, n_idx * IDX_DIM, n_idx * IDX_DIM + d_att, n_idx * IDX_DIM + d_att + 3 * d_conv
    n_h = h_up + d_pool
    assert h_q % d_att == 0 and h_a % d_conv == 0 and h_up % d_pool == 0 and h_qi % (n_idx * IDX_DIM) == 0
    n_s = d_lat + LANES

    wi = w_in.astype(BF16)
    w_h = jnp.concatenate([wi[:, :, o_qi:o_ki], wi[:, :, o_q:o_ckv], wi[:, :, o_bg:o_q], wi[:, :, o_up:o_g]], axis=2)
    w_s = jnp.concatenate([wi[:, :, o_ckv:o_qi], wi[:, :, o_ki:o_up],
                           jnp.zeros((depth, d, LANES - IDX_DIM - n_idx), BF16)], axis=2)
    w_g = wi[:, :, o_g:]
    wa_b, wb_b, wc_b = w_br_a.astype(BF16), w_br_b.astype(BF16), w_br_c.astype(BF16)
    wo_b, wup_b, wdn_b = w_o.astype(BF16), w_up.astype(BF16), w_down.astype(BF16)

    xf = x.reshape(t, d)
    xb = xf.astype(BF16)
    for l in range(depth):
        zh = _matmul(xb, w_h, l, BF16, *tl["mm"], name="proj_wide")
        zs = _matmul(xb, w_s, l, F32, tl["mm"][0], n_s, name="proj_small")
        zg = _matmul(xb, w_g, l, BF16, *tl["mm"], name="proj_gates")

        ya = _conv_a(zh, conv_a[l], bsz, seq, h_a // d_conv, d_conv, tl["rows"])
        yc = _pool(zh, pool_w[l].astype(BF16), pool_scale[l].reshape(1, d_pool), bsz, seq, h_up // d_pool,
                   tl["rows"])
        ckv = _ckv(zs, kv_norm[l].reshape(1, d_lat), d_lat, tl["rows"])
        yb = _attention(zh, zs, ckv, w_uk[l].astype(BF16), w_uv[l].astype(BF16), bsz, seq,
                        h_qi // (n_idx * IDX_DIM), h_q // d_att, n_idx, d_lat // LANES,
                        tl["tq"], min(tl["ts"], seq), *tl["slab"])

        merged = _merge(ya, yb, yc, wa_b, wb_b, wc_b, l, zg, b_gate[l], *tl["merge"])
        y1 = _matmul(merged, wo_b, l, F32, *tl["mm"], res=xf, alpha=alpha, name="proj_out")
        xf, xb = _layer_norm(y1, ln1_g[l].reshape(1, d), ln1_b[l].reshape(1, d), tl["ln_rows"])

        h = _ffn_up(xb, wup_b, l, conv_ffn_w[l], seq, *tl["ffn_up"])
        y2 = _matmul(h, wdn_b, l, F32, *tl["down"], res=xf, alpha=alpha, name="ffn_down")
        xf, xb = _layer_norm(y2, ln2_g[l].reshape(1, d), ln2_b[l].reshape(1, d), tl["ln_rows"])
    return xf.reshape(bsz, seq, d)
```

```python
import functools

import jax
import jax.numpy as jnp
from jax import lax
from jax.experimental import pallas as pl
from jax.experimental.pallas import tpu as pltpu

IDX_DIM = 64
TOPK_MAX = 256
POOL_WINDOWS = (2, 4, 8, 16)
CONV_W = 3
LN_EPS = 1e-5
RMS_EPS = 1e-6

LANES = 128
SUBLANES = 8
BF16_ROWS = 16
VMEM_BYTES_V7X = 64 * 1024 * 1024
VMEM_LIMIT = VMEM_BYTES_V7X - 8 * 1024 * 1024

HALO = BF16_ROWS
SCORE_ROWS = 128
COUNT_ROWS = 32
GROUP_UNROLL = 4
MERGE_SUB_ROWS = 256
FFN_SUB_ROWS = 256
NEG_BIG = -1e30
LOG2E = 1.4426950408889634
F32 = jnp.float32
BF16 = jnp.bfloat16


def _params(n_axes):
    return pltpu.CompilerParams(dimension_semantics=("arbitrary",) * n_axes,
                                vmem_limit_bytes=VMEM_LIMIT)


def _mm_kernel(x_ref, w_ref, o_ref):
    o_ref[...] = jnp.dot(x_ref[...], w_ref[...], preferred_element_type=F32).astype(o_ref.dtype)


def _mm_res_kernel(x_ref, w_ref, r_ref, o_ref, *, alpha):
    acc = jnp.dot(x_ref[...], w_ref[...], preferred_element_type=F32)
    o_ref[...] = (alpha * r_ref[...] + acc).astype(o_ref.dtype)


def _matmul(x, w, layer, out_dtype, tm, tn, res=None, alpha=None, name="mm"):
    m, k = x.shape
    n = w.shape[2]
    tm, tn = min(tm, m), min(tn, n)
    assert m % tm == 0 and n % tn == 0, (m, n, tm, tn)
    in_specs = [pl.BlockSpec((tm, k), lambda i, j: (i, 0)),
                pl.BlockSpec((None, k, tn), lambda i, j: (layer, 0, j))]
    args = [x, w]
    kern = _mm_kernel
    if res is not None:
        in_specs.append(pl.BlockSpec((tm, tn), lambda i, j: (i, j)))
        args.append(res)
        kern = functools.partial(_mm_res_kernel, alpha=alpha)
    return pl.pallas_call(
        kern, grid=(m // tm, n // tn), in_specs=in_specs,
        out_specs=pl.BlockSpec((tm, tn), lambda i, j: (i, j)),
        out_shape=jax.ShapeDtypeStruct((m, n), out_dtype),
        compiler_params=_params(2), name=name)(*args)


def _conv_a_kernel(bg_ref, cg_ref, v_ref, cgp_ref, vp_ref, w_ref, o_ref, buf_ref, *, rows):
    i = pl.program_id(1)
    u = cg_ref[...].astype(F32) * v_ref[...].astype(F32)
    up = cgp_ref[...].astype(F32) * vp_ref[...].astype(F32)
    buf_ref[0:HALO, :] = jnp.where(i > 0, up, 0.0)
    buf_ref[HALO:HALO + rows, :] = u
    s1 = buf_ref[HALO - 1:HALO - 1 + rows, :]
    s2 = buf_ref[HALO - 2:HALO - 2 + rows, :]
    w = w_ref[...]
    conv = w[0:1, :] * s2 + w[1:2, :] * s1 + w[2:3, :] * u
    o_ref[...] = (bg_ref[...].astype(F32) * conv).astype(o_ref.dtype)


def _conv_a(zh, conv_w, bsz, seq, col0, width, rows):
    t = zh.shape[0]
    nb = seq // rows
    hb = rows // HALO

    def cur(c):
        return pl.BlockSpec((rows, width), lambda b, i: (b * nb + i, col0 + c))

    def prev(c):
        return pl.BlockSpec((HALO, width), lambda b, i: (jnp.maximum((b * nb + i) * hb - 1, 0), col0 + c))

    return pl.pallas_call(
        functools.partial(_conv_a_kernel, rows=rows), grid=(bsz, nb),
        in_specs=[cur(0), cur(1), cur(2), prev(1), prev(2),
                  pl.BlockSpec((CONV_W, width), lambda b, i: (0, 0))],
        out_specs=pl.BlockSpec((rows, width), lambda b, i: (b * nb + i, 0)),
        out_shape=jax.ShapeDtypeStruct((t, width), BF16),
        scratch_shapes=[pltpu.VMEM((HALO + rows, width), F32)],
        compiler_params=_params(2), name="conv_a")(zh, zh, zh, zh, zh, conv_w)


def _pool_kernel(u_ref, up_ref, pw_ref, ps_ref, o_ref, buf_ref, *, rows, group):
    i = pl.program_id(1)
    ext = HALO + rows
    pos = i * rows + lax.broadcasted_iota(jnp.int32, (rows, 1), 0)
    buf_ref[0:HALO, :] = jnp.zeros((HALO, group), F32)
    for g, win in enumerate(POOL_WINDOWS):
        sl = slice(g * group, (g + 1) * group)
        u = u_ref[:, sl].astype(F32)
        buf_ref[HALO:2 * HALO, :] = jnp.where(i > 0, up_ref[:, sl].astype(F32), 0.0)
        buf_ref[2 * HALO:2 * HALO + rows, :] = u
        k = 1
        while k < win:
            s = buf_ref[HALO:HALO + ext, :] + buf_ref[HALO - k:HALO - k + ext, :]
            buf_ref[HALO:HALO + ext, :] = s
            k *= 2
        wsum = buf_ref[2 * HALO:2 * HALO + rows, :]
        cnt = jnp.minimum(pos + 1, win).astype(F32)
        d = wsum / cnt - u
        y = jnp.dot(d.astype(BF16), pw_ref[g], preferred_element_type=F32)
        o_ref[:, sl] = (y * ps_ref[:, sl]).astype(o_ref.dtype)


def _pool(zh, pool_w, pool_scale, bsz, seq, col0, rows):
    t = zh.shape[0]
    ngroups, group, _ = pool_w.shape
    width = ngroups * group
    nb = seq // rows
    hb = rows // HALO
    return pl.pallas_call(
        functools.partial(_pool_kernel, rows=rows, group=group), grid=(bsz, nb),
        in_specs=[pl.BlockSpec((rows, width), lambda b, i: (b * nb + i, col0)),
                  pl.BlockSpec((HALO, width), lambda b, i: (jnp.maximum((b * nb + i) * hb - 1, 0), col0)),
                  pl.BlockSpec((ngroups, group, group), lambda b, i: (0, 0, 0)),
                  pl.BlockSpec((1, width), lambda b, i: (0, 0))],
        out_specs=pl.BlockSpec((rows, width), lambda b, i: (b * nb + i, 0)),
        out_shape=jax.ShapeDtypeStruct((t, width), BF16),
        scratch_shapes=[pltpu.VMEM((2 * HALO + rows, group), F32)],
        compiler_params=_params(2), name="pool")(zh, zh, pool_w, pool_scale)


def _ckv_kernel(z_ref, g_ref, o_ref, *, d_lat):
    c = z_ref[:, 0:d_lat]
    ms = jnp.mean(c * c, axis=-1, keepdims=True)
    o_ref[...] = (c * lax.rsqrt(ms + RMS_EPS) * g_ref[...]).astype(o_ref.dtype)


def _ckv(zs, kv_norm, d_lat, rows):
    t, ns = zs.shape
    return pl.pallas_call(
        functools.partial(_ckv_kernel, d_lat=d_lat), grid=(t // rows,),
        in_specs=[pl.BlockSpec((rows, ns), lambda i: (i, 0)),
                  pl.BlockSpec((1, d_lat), lambda i: (0, 0))],
        out_specs=pl.BlockSpec((rows, d_lat), lambda i: (i, 0)),
        out_shape=jax.ShapeDtypeStruct((t, d_lat), BF16),
        compiler_params=_params(1), name="ckv_norm")(zs, kv_norm)


def _count_ge(key_ref, nkc, ts, tq, pred):
    def body(kc, part):
        m = pred(key_ref[:, pl.ds(pl.multiple_of(kc * ts, ts), ts)], kc).astype(F32)
        for c in range(ts // LANES):
            part = part + m[:, c * LANES:(c + 1) * LANES]
        return part

    part = lax.fori_loop(0, nkc, body, jnp.zeros((tq, LANES), F32))
    return jnp.sum(part, axis=1, keepdims=True)


def _kth_largest_key(keyt_ref, nkc, ts, tq, k):
    sign = jnp.int32(-2 ** 31)

    def count_ge(cand):
        def body(kc, part):
            m = (keyt_ref[pl.ds(pl.multiple_of(kc * ts, ts), ts), :] >= cand).astype(F32)
            for r in range(ts // COUNT_ROWS):
                part = part + m[r * COUNT_ROWS:(r + 1) * COUNT_ROWS, :]
            return part

        part = lax.fori_loop(0, nkc, body, jnp.zeros((COUNT_ROWS, tq), F32))
        return jnp.sum(part, axis=0, keepdims=True)

    def body(b, carry):
        v, n_v = carry
        cand = v | lax.shift_left(jnp.int32(1), jnp.int32(31) - b)
        n_cand = count_ge(cand ^ sign)
        keep = n_cand >= k
        return jnp.where(keep, cand, v), jnp.where(keep, n_cand, n_v)

    n_all = jnp.broadcast_to((nkc * ts).astype(F32), (1, tq))
    v, n_v = lax.fori_loop(0, 32, body, (jnp.zeros((1, tq), jnp.int32), n_all))
    return v ^ sign, n_v


def _bisect_largest(nbits, target, count_fn, tq):
    sign = jnp.int32(-2 ** 31) if nbits == 32 else jnp.int32(0)

    def body(b, v):
        bit = lax.shift_left(jnp.int32(1), jnp.int32(nbits - 1) - b)
        cand = v | bit
        cnt = count_fn(cand ^ sign)
        return jnp.where(cnt >= target, cand, v)

    v = lax.fori_loop(0, nbits, body, jnp.zeros((tq, 1), jnp.int32))
    return v ^ sign


def _attn_kernel(qi_ref, zq_ref, zk_ref, ckv_ref, q_ref, wuk_ref, wuv_ref, o_ref,
                 key_ref, keyt_ref, cut_ref, bias_ref, qs_ref, sc_ref, cmax_ref, m_ref, l_ref, a_ref, acc_ref, *, tq, ts,
                 seq, n_idx,
                 n_heads, d_lat, head_dim, k_sel, idx_scale, sm_scale, slab, grp):
    i = pl.program_id(1)
    q0 = i * tq
    nkc = (q0 + tq + ts - 1) // ts
    nbits_idx = max(1, (seq - 1).bit_length())
    row = q0 + lax.broadcasted_iota(jnp.int32, (tq, 1), 0)
    int_min = jnp.int32(-2 ** 31)

    wq = zq_ref[:, IDX_DIM:IDX_DIM + n_idx] * idx_scale

    def score_body(kc, carry):
        s0 = pl.multiple_of(kc * ts, ts)
        ki = zk_ref[pl.ds(s0, ts), 0:IDX_DIM].astype(BF16)
        col = s0 + lax.broadcasted_iota(jnp.int32, (SCORE_ROWS, ts), 1)
        for r0 in range(0, tq, SCORE_ROWS):
            acc = jnp.zeros((SCORE_ROWS, ts), F32)
            for h in range(n_idx):
                lg = lax.dot_general(qi_ref[r0:r0 + SCORE_ROWS, h * IDX_DIM:(h + 1) * IDX_DIM], ki,
                                     (((1,), (1,)), ((), ())), preferred_element_type=F32)
                acc = acc + wq[r0:r0 + SCORE_ROWS, h:h + 1] * jnp.maximum(lg, 0.0)
            bits = lax.bitcast_convert_type(acc, jnp.int32)
            key = bits ^ (lax.shift_right_arithmetic(bits, 31) & jnp.int32(0x7FFFFFFF))
            key = jnp.where(col <= row[r0:r0 + SCORE_ROWS], key, int_min)
            key_ref[r0:r0 + SCORE_ROWS, pl.ds(s0, ts)] = key
            keyt_ref[pl.ds(s0, ts), r0:r0 + SCORE_ROWS] = key.T
        return carry

    lax.fori_loop(0, nkc, score_body, 0)

    kf = jnp.float32(k_sel)
    thr_t, n_ge = _kth_largest_key(keyt_ref, nkc, ts, tq, kf)
    thr = jnp.broadcast_to(thr_t, (SUBLANES, tq)).T[:, 0:1]

    def rev_idx(kc):
        return (seq - 1) - (kc * ts + lax.broadcasted_iota(jnp.int32, (tq, ts), 1))

    cut_ref[...] = jnp.zeros((tq, LANES), jnp.int32)

    @pl.when(jnp.max(n_ge) > kf)
    def _():
        n_gt = _count_ge(key_ref, nkc, ts, tq, lambda x, kc: x > thr)
        need = kf - n_gt
        c = _bisect_largest(
            nbits_idx, need,
            lambda v: _count_ge(key_ref, nkc, ts, tq, lambda x, kc: (x == thr) & (rev_idx(kc) >= v)), tq)
        cut_ref[...] = jnp.broadcast_to(c, (tq, LANES))

    cut = cut_ref[:, 0:1]

    rows_all = n_heads * tq
    nlc = ts // LANES
    k2 = sm_scale * LOG2E
    m_ref[...] = jnp.full((rows_all, LANES), NEG_BIG, F32)
    l_ref[...] = jnp.zeros((rows_all, LANES), F32)
    acc_ref[...] = jnp.zeros((rows_all, d_lat), F32)
    for h in range(n_heads):
        qs_ref[h * tq:(h + 1) * tq, :] = jnp.dot(q_ref[:, h * head_dim:(h + 1) * head_dim], wuk_ref[h],
                                                 preferred_element_type=F32).astype(BF16)
    n_groups = rows_all // grp
    unroll = min(GROUP_UNROLL, n_groups)

    def chunk_body(kc, carry):
        s0 = pl.multiple_of(kc * ts, ts)
        x = key_ref[:, pl.ds(s0, ts)]
        col = s0 + lax.broadcasted_iota(jnp.int32, (tq, ts), 1)
        sel = ((x > thr) | ((x == thr) & (rev_idx(kc) >= cut))) & (col <= row)
        bias_ref[...] = jnp.where(sel, 0.0, NEG_BIG)
        ckv = ckv_ref[pl.ds(s0, ts), :]

        def qk_body(g, c2):
            g0 = pl.multiple_of(g * grp, grp)
            sc = lax.dot_general(qs_ref[pl.ds(g0, grp), :], ckv, (((1,), (1,)), ((), ())),
                                 preferred_element_type=F32)
            for s in range(grp // slab):
                r0, b0 = s * slab, (s * slab) % tq
                rows_s = pl.ds(g0 + r0, slab)
                xs = [sc[r0:r0 + slab, c * LANES:(c + 1) * LANES] * k2
                      + bias_ref[b0:b0 + slab, c * LANES:(c + 1) * LANES] for c in range(nlc)]
                for c in range(nlc):
                    sc_ref[rows_s, c * LANES:(c + 1) * LANES] = xs[c]
                cmax_ref[rows_s, :] = functools.reduce(jnp.maximum, xs)
            return c2

        lax.fori_loop(0, n_groups, qk_body, 0, unroll=unroll)

        m_old = m_ref[...]
        m_new = jnp.maximum(m_old, jnp.max(cmax_ref[...], axis=1, keepdims=True))
        a_ref[...] = jnp.exp2(m_old - m_new)
        m_ref[...] = m_new

        def pv_body(g, c2):
            g0 = pl.multiple_of(g * grp, grp)
            probs = []
            for s in range(grp // slab):
                rows_s = pl.ds(g0 + s * slab, slab)
                m = m_ref[rows_s, :]
                ps = [jnp.exp2(sc_ref[rows_s, c * LANES:(c + 1) * LANES] - m) for c in range(nlc)]
                l_ref[rows_s, :] = a_ref[rows_s, :] * l_ref[rows_s, :] + functools.reduce(lambda a, b: a + b, ps)
                probs.append(jnp.concatenate([pc.astype(BF16) for pc in ps], axis=1))
            pv = jnp.dot(jnp.concatenate(probs, axis=0), ckv, preferred_element_type=F32)
            a = a_ref[pl.ds(g0, grp), :]
            acc_ref[pl.ds(g0, grp), :] = (acc_ref[pl.ds(g0, grp), :] * jnp.concatenate([a] * (d_lat // LANES), axis=1)
                                          + pv)
            return c2

        lax.fori_loop(0, n_groups, pv_body, 0, unroll=unroll)
        return carry

    lax.fori_loop(0, nkc, chunk_body, 0)

    for h in range(n_heads):
        l = jnp.sum(l_ref[h * tq:(h + 1) * tq, :], axis=1, keepdims=True)
        o_lat = (acc_ref[h * tq:(h + 1) * tq, :] / l).astype(BF16)
        o_ref[:, h * head_dim:(h + 1) * head_dim] = jnp.dot(
            o_lat, wuv_ref[h], preferred_element_type=F32).astype(o_ref.dtype)


def _attention(zh, zs, ckv, w_uk, w_uv, bsz, seq, qi_col, q_col, n_idx, kw_col, tq, ts, slab, grp):
    t = zh.shape[0]
    nh, dl, hd = w_uv.shape
    nq = seq // tq
    k_sel = min(TOPK_MAX, seq // 4)
    assert ts >= k_sel and seq % ts == 0 and seq % tq == 0 and ts % tq == 0
    assert grp % tq == 0 and (nh * tq) % grp == 0 and tq % slab == 0 and slab % BF16_ROWS == 0
    assert tq % SCORE_ROWS == 0
    kern = functools.partial(
        _attn_kernel, tq=tq, ts=ts, seq=seq, n_idx=n_idx, n_heads=nh, d_lat=dl, head_dim=hd, k_sel=k_sel,
        idx_scale=float(IDX_DIM ** -0.5 * n_idx ** -0.5), sm_scale=float(hd ** -0.5), slab=slab, grp=grp)
    return pl.pallas_call(
        kern, grid=(bsz, nq),
        in_specs=[pl.BlockSpec((tq, n_idx * IDX_DIM), lambda b, i: (b * nq + i, qi_col)),
                  pl.BlockSpec((tq, LANES), lambda b, i: (b * nq + i, kw_col)),
                  pl.BlockSpec((seq, LANES), lambda b, i: (b, kw_col)),
                  pl.BlockSpec((seq, dl), lambda b, i: (b, 0)),
                  pl.BlockSpec((tq, nh * hd), lambda b, i: (b * nq + i, q_col)),
                  pl.BlockSpec((nh, hd, dl), lambda b, i: (0, 0, 0)),
                  pl.BlockSpec((nh, dl, hd), lambda b, i: (0, 0, 0))],
        out_specs=pl.BlockSpec((tq, nh * hd), lambda b, i: (b * nq + i, 0)),
        out_shape=jax.ShapeDtypeStruct((t, nh * hd), BF16),
        scratch_shapes=[pltpu.VMEM((tq, seq), jnp.int32),
                        pltpu.VMEM((seq, tq), jnp.int32),
                        pltpu.VMEM((tq, LANES), jnp.int32),
                        pltpu.VMEM((tq, ts), F32),
                        pltpu.VMEM((nh * tq, dl), BF16),
                        pltpu.VMEM((nh * tq, ts), F32),
                        pltpu.VMEM((nh * tq, LANES), F32),
                        pltpu.VMEM((nh * tq, LANES), F32),
                        pltpu.VMEM((nh * tq, LANES), F32),
                        pltpu.VMEM((nh * tq, LANES), F32),
                        pltpu.VMEM((nh * tq, dl), F32)],
        compiler_params=_params(2), name="dsa_attention")(zh, zs, zs, ckv, zh, w_uk, w_uv)


def _merge_kernel(ya_ref, yb_ref, yc_ref, wa_ref, wb_ref, wc_ref, g0_ref, g1_ref, g2_ref, b_ref, o_ref):
    b = b_ref[...]
    tm = o_ref.shape[0]
    sub = min(tm, MERGE_SUB_ROWS)
    for r0 in range(0, tm, sub):
        out = None
        for k, (y_ref, w_ref, g_ref) in enumerate(((ya_ref, wa_ref, g0_ref), (yb_ref, wb_ref, g1_ref),
                                                    (yc_ref, wc_ref, g2_ref))):
            gate = jax.nn.sigmoid(g_ref[r0:r0 + sub, :].astype(F32) + b[k:k + 1, :])
            term = gate * jnp.dot(y_ref[r0:r0 + sub, :], w_ref[...], preferred_element_type=F32)
            out = term if out is None else out + term
        o_ref[r0:r0 + sub, :] = out.astype(o_ref.dtype)


def _merge(ya, yb, yc, wa, wb, wc, layer, zg, b_gate, tm, tn):
    t = ya.shape[0]
    d = wa.shape[2]
    nj = d // tn

    def yspec(y):
        return pl.BlockSpec((tm, y.shape[1]), lambda i, j: (i, 0))

    def wspec(w):
        return pl.BlockSpec((None, w.shape[1], tn), lambda i, j: (layer, 0, j))

    def gspec(k):
        return pl.BlockSpec((tm, tn), lambda i, j: (i, k * nj + j))

    return pl.pallas_call(
        _merge_kernel, grid=(t // tm, nj),
        in_specs=[yspec(ya), yspec(yb), yspec(yc), wspec(wa), wspec(wb), wspec(wc),
                  gspec(0), gspec(1), gspec(2), pl.BlockSpec((b_gate.shape[0], tn), lambda i, j: (0, j))],
        out_specs=pl.BlockSpec((tm, tn), lambda i, j: (i, j)),
        out_shape=jax.ShapeDtypeStruct((t, d), BF16),
        compiler_params=_params(2), name="merge")(ya, yb, yc, wa, wb, wc, zg, zg, zg, b_gate)


def _ln_kernel(y_ref, g_ref, b_ref, o_ref, ob_ref):
    y = y_ref[...]
    mu = jnp.mean(y, axis=-1, keepdims=True)
    yc = y - mu
    var = jnp.mean(yc * yc, axis=-1, keepdims=True)
    out = yc * lax.rsqrt(var + LN_EPS) * g_ref[...] + b_ref[...]
    o_ref[...] = out
    ob_ref[...] = out.astype(BF16)


def _layer_norm(y, g, b, rows):
    t, d = y.shape
    row_spec = pl.BlockSpec((rows, d), lambda i: (i, 0))
    vec_spec = pl.BlockSpec((1, d), lambda i: (0, 0))
    return pl.pallas_call(
        _ln_kernel, grid=(t // rows,), in_specs=[row_spec, vec_spec, vec_spec],
        out_specs=[row_spec, row_spec],
        out_shape=[jax.ShapeDtypeStruct((t, d), F32), jax.ShapeDtypeStruct((t, d), BF16)],
        compiler_params=_params(1), name="layer_norm")(y, g, b)


def _ffn_up_kernel(x_ref, wg_ref, wu_ref, cw_ref, o_ref, buf_ref, carry_ref, *, tm, sub, tiles_per_seq):
    i = pl.program_id(0)
    j = pl.program_id(1)
    seq_start = i % tiles_per_seq == 0

    @pl.when(seq_start)
    def _():
        buf_ref[0:SUBLANES, :] = jnp.zeros((SUBLANES, buf_ref.shape[1]), F32)

    @pl.when(jnp.logical_not(seq_start))
    def _():
        buf_ref[0:SUBLANES, :] = carry_ref[j]

    w = cw_ref[...]
    for r0 in range(0, tm, sub):
        xs = x_ref[r0:r0 + sub, :]
        gt = jnp.dot(xs, wg_ref[...], preferred_element_type=F32)
        up = jnp.dot(xs, wu_ref[...], preferred_element_type=F32)
        buf_ref[SUBLANES + r0:SUBLANES + r0 + sub, :] = gt
        if r0 + sub == tm:
            carry_ref[j] = gt[sub - SUBLANES:sub, :]
        s1 = buf_ref[SUBLANES - 1 + r0:SUBLANES - 1 + r0 + sub, :]
        s2 = buf_ref[SUBLANES - 2 + r0:SUBLANES - 2 + r0 + sub, :]
        conv = w[0:1, :] * s2 + w[1:2, :] * s1 + w[2:3, :] * gt
        o_ref[r0:r0 + sub, :] = (jax.nn.silu(conv) * up).astype(o_ref.dtype)


def _ffn_up(xb, w_up, layer, conv_w, seq, tm, tn):
    t, d = xb.shape
    dff = conv_w.shape[1]
    assert seq % tm == 0 and dff % tn == 0 and w_up.shape[2] == 2 * dff
    nj = dff // tn
    return pl.pallas_call(
        functools.partial(_ffn_up_kernel, tm=tm, sub=min(tm, FFN_SUB_ROWS), tiles_per_seq=seq // tm),
        grid=(t // tm, nj),
        in_specs=[pl.BlockSpec((tm, d), lambda i, j: (i, 0)),
                  pl.BlockSpec((None, d, tn), lambda i, j: (layer, 0, j)),
                  pl.BlockSpec((None, d, tn), lambda i, j: (layer, 0, nj + j)),
                  pl.BlockSpec((CONV_W, tn), lambda i, j: (0, j))],
        out_specs=pl.BlockSpec((tm, tn), lambda i, j: (i, j)),
        out_shape=jax.ShapeDtypeStruct((t, dff), BF16),
        scratch_shapes=[pltpu.VMEM((SUBLANES + tm, tn), F32), pltpu.VMEM((nj, SUBLANES, tn), F32)],
        compiler_params=_params(2), name="ffn_up")(xb, w_up, w_up, conv_w)


def _tiles(seq, d_model):
    big = d_model >= 4096
    return dict(
        mm=(1024, 1024) if big else (256, 512),
        down=(512, 512) if big else (256, 512),
        ffn_up=(2048, 256) if big else (256, 256),
        merge=(1024, 512) if big else (256, 512),
        rows=512 if big else 256,
        ln_rows=256,
        tq=128, ts=512, slab=(32, 512))


def kernel(x, w_in, b_gate, conv_a, kv_norm, w_uk, w_uv, pool_w, pool_scale, w_br_a, w_br_b, w_br_c,
           w_o, ln1_g, ln1_b, w_up, conv_ffn_w, w_down, ln2_g, ln2_b):
    bsz, seq, d = x.shape
    depth = w_in.shape[0]
    alpha = (2.0 * depth) ** 0.25
    d_conv = conv_a.shape[2]
    n_heads, head_dim, d_lat = w_uk.shape[1:]
    d_att = n_heads * head_dim
    d_pool = pool_scale.shape[1]
    d_ff = conv_ffn_w.shape[2]
    n_in = w_in.shape[2]
    n_idx = (n_in - (3 * d_conv + d_att + d_lat + IDX_DIM + d_pool + b_gate.shape[1] * d)) // (IDX_DIM + 1)
    t = bsz * seq
    tl = _tiles(seq, d)

    o_bg, o_cg, o_v = 0, d_conv, 2 * d_conv
    o_q = 3 * d_conv
    o_ckv = o_q + d_att
    o_qi = o_ckv + d_lat
    o_ki = o_qi + n_idx * IDX_DIM
    o_wi = o_ki + IDX_DIM
    o_up = o_wi + n_idx
    o_g = o_up + d_pool
    assert o_g + b_gate.shape[1] * d == n_in
    assert IDX_DIM + n_idx <= LANES

    h_qi, h_q, h_a, h_up = 0, n_idx * IDX_DIM, n_idx * IDX_DIM + d_att, n_idx * IDX_DIM + d_att + 3 * d_conv
    n_h = h_up + d_pool
    assert h_q % d_att == 0 and h_a % d_conv == 0 and h_up % d_pool == 0 and h_qi % (n_idx * IDX_DIM) == 0
    n_s = d_lat + LANES

    wi = w_in.astype(BF16)
    w_h = jnp.concatenate([wi[:, :, o_qi:o_ki], wi[:, :, o_q:o_ckv], wi[:, :, o_bg:o_q], wi[:, :, o_up:o_g]], axis=2)
    w_s = jnp.concatenate([wi[:, :, o_ckv:o_qi], wi[:, :, o_ki:o_up],
                           jnp.zeros((depth, d, LANES - IDX_DIM - n_idx), BF16)], axis=2)
    w_g = wi[:, :, o_g:]
    wa_b, wb_b, wc_b = w_br_a.astype(BF16), w_br_b.astype(BF16), w_br_c.astype(BF16)
    wo_b, wup_b, wdn_b = w_o.astype(BF16), w_up.astype(BF16), w_down.astype(BF16)

    xf = x.reshape(t, d)
    xb = xf.astype(BF16)
    for l in range(depth):
        zh = _matmul(xb, w_h, l, BF16, *tl["mm"], name="proj_wide")
        zs = _matmul(xb, w_s, l, F32, tl["mm"][0], n_s, name="proj_small")
        zg = _matmul(xb, w_g, l, BF16, *tl["mm"], name="proj_gates")

        ya = _conv_a(zh, conv_a[l], bsz, seq, h_a // d_conv, d_conv, tl["rows"])
        yc = _pool(zh, pool_w[l].astype(BF16), pool_scale[l].reshape(1, d_pool), bsz, seq, h_up // d_pool,
                   tl["rows"])
        ckv = _ckv(zs, kv_norm[l].reshape(1, d_lat), d_lat, tl["rows"])
        yb = _attention(zh, zs, ckv, w_uk[l].astype(BF16), w_uv[l].astype(BF16), bsz, seq,
                        h_qi // (n_idx * IDX_DIM), h_q // d_att, n_idx, d_lat // LANES,
                        tl["tq"], min(tl["ts"], seq), *tl["slab"])

        merged = _merge(ya, yb, yc, wa_b, wb_b, wc_b, l, zg, b_gate[l], *tl["merge"])
        y1 = _matmul(merged, wo_b, l, F32, *tl["mm"], res=xf, alpha=alpha, name="proj_out")
        xf, xb = _layer_norm(y1, ln1_g[l].reshape(1, d), ln1_b[l].reshape(1, d), tl["ln_rows"])

        h = _ffn_up(xb, wup_b, l, conv_ffn_w[l], seq, *tl["ffn_up"])
        y2 = _matmul(h, wdn_b, l, F32, *tl["down"], res=xf, alpha=alpha, name="ffn_down")
        xf, xb = _layer_norm(y2, ln2_g[l].reshape(1, d), ln2_b[l].reshape(1, d), tl["ln_rows"])
    return xf.reshape(bsz, seq, d)
```

```python
import functools

import jax
import jax.numpy as jnp
from jax import lax
from jax.experimental import pallas as pl
from jax.experimental.pallas import tpu as pltpu

IDX_DIM = 64
TOPK_MAX = 256
POOL_WINDOWS = (2, 4, 8, 16)
CONV_W = 3
LN_EPS = 1e-5
RMS_EPS = 1e-6

LANES = 128
SUBLANES = 8
BF16_ROWS = 16
VMEM_BYTES_V7X = 64 * 1024 * 1024
VMEM_LIMIT = VMEM_BYTES_V7X - 8 * 1024 * 1024

HALO = BF16_ROWS
SCORE_ROWS = 128
COUNT_ROWS = 32
GROUP_UNROLL = 4
MERGE_SUB_ROWS = 256
FFN_SUB_ROWS = 256
NEG_BIG = -1e30
LOG2E = 1.4426950408889634
F32 = jnp.float32
BF16 = jnp.bfloat16


def _params(n_axes):
    return pltpu.CompilerParams(dimension_semantics=("arbitrary",) * n_axes,
                                vmem_limit_bytes=VMEM_LIMIT)


def _mm_kernel(x_ref, w_ref, o_ref):
    o_ref[...] = jnp.dot(x_ref[...], w_ref[...], preferred_element_type=F32).astype(o_ref.dtype)


def _mm_res_kernel(x_ref, w_ref, r_ref, o_ref, *, alpha):
    acc = jnp.dot(x_ref[...], w_ref[...], preferred_element_type=F32)
    o_ref[...] = (alpha * r_ref[...] + acc).astype(o_ref.dtype)


def _matmul(x, w, layer, out_dtype, tm, tn, res=None, alpha=None, name="mm"):
    m, k = x.shape
    n = w.shape[2]
    tm, tn = min(tm, m), min(tn, n)
    assert m % tm == 0 and n % tn == 0, (m, n, tm, tn)
    in_specs = [pl.BlockSpec((tm, k), lambda i, j: (i, 0)),
                pl.BlockSpec((None, k, tn), lambda i, j: (layer, 0, j))]
    args = [x, w]
    kern = _mm_kernel
    if res is not None:
        in_specs.append(pl.BlockSpec((tm, tn), lambda i, j: (i, j)))
        args.append(res)
        kern = functools.partial(_mm_res_kernel, alpha=alpha)
    return pl.pallas_call(
        kern, grid=(m // tm, n // tn), in_specs=in_specs,
        out_specs=pl.BlockSpec((tm, tn), lambda i, j: (i, j)),
        out_shape=jax.ShapeDtypeStruct((m, n), out_dtype),
        compiler_params=_params(2), name=name)(*args)


def _conv_a_kernel(bg_ref, cg_ref, v_ref, cgp_ref, vp_ref, w_ref, o_ref, buf_ref, *, rows):
    i = pl.program_id(1)
    u = cg_ref[...].astype(F32) * v_ref[...].astype(F32)
    up = cgp_ref[...].astype(F32) * vp_ref[...].astype(F32)
    buf_ref[0:HALO, :] = jnp.where(i > 0, up, 0.0)
    buf_ref[HALO:HALO + rows, :] = u
    s1 = buf_ref[HALO - 1:HALO - 1 + rows, :]
    s2 = buf_ref[HALO - 2:HALO - 2 + rows, :]
    w = w_ref[...]
    conv = w[0:1, :] * s2 + w[1:2, :] * s1 + w[2:3, :] * u
    o_ref[...] = (bg_ref[...].astype(F32) * conv).astype(o_ref.dtype)


def _conv_a(zh, conv_w, bsz, seq, col0, width, rows):
    t = zh.shape[0]
    nb = seq // rows
    hb = rows // HALO

    def cur(c):
        return pl.BlockSpec((rows, width), lambda b, i: (b * nb + i, col0 + c))

    def prev(c):
        return pl.BlockSpec((HALO, width), lambda b, i: (jnp.maximum((b * nb + i) * hb - 1, 0), col0 + c))

    return pl.pallas_call(
        functools.partial(_conv_a_kernel, rows=rows), grid=(bsz, nb),
        in_specs=[cur(0), cur(1), cur(2), prev(1), prev(2),
                  pl.BlockSpec((CONV_W, width), lambda b, i: (0, 0))],
        out_specs=pl.BlockSpec((rows, width), lambda b, i: (b * nb + i, 0)),
        out_shape=jax.ShapeDtypeStruct((t, width), BF16),
        scratch_shapes=[pltpu.VMEM((HALO + rows, width), F32)],
        compiler_params=_params(2), name="conv_a")(zh, zh, zh, zh, zh, conv_w)


def _pool_kernel(u_ref, up_ref, pw_ref, ps_ref, o_ref, buf_ref, *, rows, group):
    i = pl.program_id(1)
    ext = HALO + rows
    pos = i * rows + lax.broadcasted_iota(jnp.int32, (rows, 1), 0)
    buf_ref[0:HALO, :] = jnp.zeros((HALO, group), F32)
    for g, win in enumerate(POOL_WINDOWS):
        sl = slice(g * group, (g + 1) * group)
        u = u_ref[:, sl].astype(F32)
        buf_ref[HALO:2 * HALO, :] = jnp.where(i > 0, up_ref[:, sl].astype(F32), 0.0)
        buf_ref[2 * HALO:2 * HALO + rows, :] = u
        k = 1
        while k < win:
            s = buf_ref[HALO:HALO + ext, :] + buf_ref[HALO - k:HALO - k + ext, :]
            buf_ref[HALO:HALO + ext, :] = s
            k *= 2
        wsum = buf_ref[2 * HALO:2 * HALO + rows, :]
        cnt = jnp.minimum(pos + 1, win).astype(F32)
        d = wsum / cnt - u
        y = jnp.dot(d.astype(BF16), pw_ref[g], preferred_element_type=F32)
        o_ref[:, sl] = (y * ps_ref[:, sl]).astype(o_ref.dtype)


def _pool(zh, pool_w, pool_scale, bsz, seq, col0, rows):
    t = zh.shape[0]
    ngroups, group, _ = pool_w.shape
    width = ngroups * group
    nb = seq // rows
    hb = rows // HALO
    return pl.pallas_call(
        functools.partial(_pool_kernel, rows=rows, group=group), grid=(bsz, nb),
        in_specs=[pl.BlockSpec((rows, width), lambda b, i: (b * nb + i, col0)),
                  pl.BlockSpec((HALO, width), lambda b, i: (jnp.maximum((b * nb + i) * hb - 1, 0), col0)),
                  pl.BlockSpec((ngroups, group, group), lambda b, i: (0, 0, 0)),
                  pl.BlockSpec((1, width), lambda b, i: (0, 0))],
        out_specs=pl.BlockSpec((rows, width), lambda b, i: (b * nb + i, 0)),
        out_shape=jax.ShapeDtypeStruct((t, width), BF16),
        scratch_shapes=[pltpu.VMEM((2 * HALO + rows, group), F32)],
        compiler_params=_params(2), name="pool")(zh, zh, pool_w, pool_scale)


def _ckv_kernel(z_ref, g_ref, o_ref, *, d_lat):
    c = z_ref[:, 0:d_lat]
    ms = jnp.mean(c * c, axis=-1, keepdims=True)
    o_ref[...] = (c * lax.rsqrt(ms + RMS_EPS) * g_ref[...]).astype(o_ref.dtype)


def _ckv(zs, kv_norm, d_lat, rows):
    t, ns = zs.shape
    return pl.pallas_call(
        functools.partial(_ckv_kernel, d_lat=d_lat), grid=(t // rows,),
        in_specs=[pl.BlockSpec((rows, ns), lambda i: (i, 0)),
                  pl.BlockSpec((1, d_lat), lambda i: (0, 0))],
        out_specs=pl.BlockSpec((rows, d_lat), lambda i: (i, 0)),
        out_shape=jax.ShapeDtypeStruct((t, d_lat), BF16),
        compiler_params=_params(1), name="ckv_norm")(zs, kv_norm)


def _count_ge(key_ref, nkc, ts, tq, pred):
    def body(kc, part):
        m = pred(key_ref[:, pl.ds(pl.multiple_of(kc * ts, ts), ts)], kc).astype(F32)
        for c in range(ts // LANES):
            part = part + m[:, c * LANES:(c + 1) * LANES]
        return part

    part = lax.fori_loop(0, nkc, body, jnp.zeros((tq, LANES), F32))
    return jnp.sum(part, axis=1, keepdims=True)


def _kth_largest_key(keyt_ref, nkc, ts, tq, k):
    sign = jnp.int32(-2 ** 31)

    def count_ge(cand):
        def body(kc, part):
            m = (keyt_ref[pl.ds(pl.multiple_of(kc * ts, ts), ts), :] >= cand).astype(F32)
            for r in range(ts // COUNT_ROWS):
                part = part + m[r * COUNT_ROWS:(r + 1) * COUNT_ROWS, :]
            return part

        part = lax.fori_loop(0, nkc, body, jnp.zeros((COUNT_ROWS, tq), F32))
        return jnp.sum(part, axis=0, keepdims=True)

    def body(b, carry):
        v, n_v = carry
        cand = v | lax.shift_left(jnp.int32(1), jnp.int32(31) - b)
        n_cand = count_ge(cand ^ sign)
        keep = n_cand >= k
        return jnp.where(keep, cand, v), jnp.where(keep, n_cand, n_v)

    n_all = jnp.broadcast_to((nkc * ts).astype(F32), (1, tq))
    v, n_v = lax.fori_loop(0, 32, body, (jnp.zeros((1, tq), jnp.int32), n_all))
    return v ^ sign, n_v


def _bisect_largest(nbits, target, count_fn, tq):
    sign = jnp.int32(-2 ** 31) if nbits == 32 else jnp.int32(0)

    def body(b, v):
        bit = lax.shift_left(jnp.int32(1), jnp.int32(nbits - 1) - b)
        cand = v | bit
        cnt = count_fn(cand ^ sign)
        return jnp.where(cnt >= target, cand, v)

    v = lax.fori_loop(0, nbits, body, jnp.zeros((tq, 1), jnp.int32))
    return v ^ sign


def _attn_kernel(qi_ref, zq_ref, zk_ref, ckv_ref, q_ref, wuk_ref, wuv_ref, o_ref,
                 key_ref, keyt_ref, cut_ref, bias_ref, qs_ref, sc_ref, cmax_ref, m_ref, l_ref, a_ref, acc_ref, *, tq, ts,
                 seq, n_idx,
                 n_heads, d_lat, head_dim, k_sel, idx_scale, sm_scale, slab, grp):
    i = pl.program_id(1)
    q0 = i * tq
    nkc = (q0 + tq + ts - 1) // ts
    nbits_idx = max(1, (seq - 1).bit_length())
    row = q0 + lax.broadcasted_iota(jnp.int32, (tq, 1), 0)
    int_min = jnp.int32(-2 ** 31)

    wq = zq_ref[:, IDX_DIM:IDX_DIM + n_idx] * idx_scale

    def score_body(kc, carry):
        s0 = pl.multiple_of(kc * ts, ts)
        ki = zk_ref[pl.ds(s0, ts), 0:IDX_DIM].astype(BF16)
        col = s0 + lax.broadcasted_iota(jnp.int32, (SCORE_ROWS, ts), 1)
        for r0 in range(0, tq, SCORE_ROWS):
            acc = jnp.zeros((SCORE_ROWS, ts), F32)
            for h in range(n_idx):
                lg = lax.dot_general(qi_ref[r0:r0 + SCORE_ROWS, h * IDX_DIM:(h + 1) * IDX_DIM], ki,
                                     (((1,), (1,)), ((), ())), preferred_element_type=F32)
                acc = acc + wq[r0:r0 + SCORE_ROWS, h:h + 1] * jnp.maximum(lg, 0.0)
            bits = lax.bitcast_convert_type(acc, jnp.int32)
            key = bits ^ (lax.shift_right_arithmetic(bits, 31) & jnp.int32(0x7FFFFFFF))
            key = jnp.where(col <= row[r0:r0 + SCORE_ROWS], key, int_min)
            key_ref[r0:r0 + SCORE_ROWS, pl.ds(s0, ts)] = key
            keyt_ref[pl.ds(s0, ts), r0:r0 + SCORE_ROWS] = key.T
        return carry

    lax.fori_loop(0, nkc, score_body, 0)

    kf = jnp.float32(k_sel)
    thr_t, n_ge = _kth_largest_key(keyt_ref, nkc, ts, tq, kf)
    thr = jnp.broadcast_to(thr_t, (SUBLANES, tq)).T[:, 0:1]

    def rev_idx(kc):
        return (seq - 1) - (kc * ts + lax.broadcasted_iota(jnp.int32, (tq, ts), 1))

    cut_ref[...] = jnp.zeros((tq, LANES), jnp.int32)

    @pl.when(jnp.max(jnp.where(thr_t > int_min, n_ge, 0.0)) > kf)
    def _():
        n_gt = _count_ge(key_ref, nkc, ts, tq, lambda x, kc: x > thr)
        need = kf - n_gt
        c = _bisect_largest(
            nbits_idx, need,
            lambda v: _count_ge(key_ref, nkc, ts, tq, lambda x, kc: (x == thr) & (rev_idx(kc) >= v)), tq)
        cut_ref[...] = jnp.broadcast_to(c, (tq, LANES))

    cut = cut_ref[:, 0:1]

    rows_all = n_heads * tq
    nlc = ts // LANES
    k2 = sm_scale * LOG2E
    m_ref[...] = jnp.full((rows_all, LANES), NEG_BIG, F32)
    l_ref[...] = jnp.zeros((rows_all, LANES), F32)
    acc_ref[...] = jnp.zeros((rows_all, d_lat), F32)
    for h in range(n_heads):
        qs_ref[h * tq:(h + 1) * tq, :] = jnp.dot(q_ref[:, h * head_dim:(h + 1) * head_dim], wuk_ref[h],
                                                 preferred_element_type=F32).astype(BF16)
    n_groups = rows_all // grp
    unroll = min(GROUP_UNROLL, n_groups)

    def chunk_body(kc, carry):
        s0 = pl.multiple_of(kc * ts, ts)
        x = key_ref[:, pl.ds(s0, ts)]
        col = s0 + lax.broadcasted_iota(jnp.int32, (tq, ts), 1)
        sel = ((x > thr) | ((x == thr) & (rev_idx(kc) >= cut))) & (col <= row)
        bias_ref[...] = jnp.where(sel, 0.0, NEG_BIG)
        ckv = ckv_ref[pl.ds(s0, ts), :]

        def qk_body(g, c2):
            g0 = pl.multiple_of(g * grp, grp)
            sc = lax.dot_general(qs_ref[pl.ds(g0, grp), :], ckv, (((1,), (1,)), ((), ())),
                                 preferred_element_type=F32)
            for s in range(grp // slab):
                r0, b0 = s * slab, (s * slab) % tq
                rows_s = pl.ds(g0 + r0, slab)
                xs = [sc[r0:r0 + slab, c * LANES:(c + 1) * LANES] * k2
                      + bias_ref[b0:b0 + slab, c * LANES:(c + 1) * LANES] for c in range(nlc)]
                for c in range(nlc):
                    sc_ref[rows_s, c * LANES:(c + 1) * LANES] = xs[c]
                cmax_ref[rows_s, :] = functools.reduce(jnp.maximum, xs)
            return c2

        lax.fori_loop(0, n_groups, qk_body, 0, unroll=unroll)

        m_old = m_ref[...]
        m_new = jnp.maximum(m_old, jnp.max(cmax_ref[...], axis=1, keepdims=True))
        a_ref[...] = jnp.exp2(m_old - m_new)
        m_ref[...] = m_new

        def pv_body(g, c2):
            g0 = pl.multiple_of(g * grp, grp)
            probs = []
            for s in range(grp // slab):
                rows_s = pl.ds(g0 + s * slab, slab)
                m = m_ref[rows_s, :]
                ps = [jnp.exp2(sc_ref[rows_s, c * LANES:(c + 1) * LANES] - m) for c in range(nlc)]
                l_ref[rows_s, :] = a_ref[rows_s, :] * l_ref[rows_s, :] + functools.reduce(lambda a, b: a + b, ps)
                probs.append(jnp.concatenate([pc.astype(BF16) for pc in ps], axis=1))
            pv = jnp.dot(jnp.concatenate(probs, axis=0), ckv, preferred_element_type=F32)
            a = a_ref[pl.ds(g0, grp), :]
            acc_ref[pl.ds(g0, grp), :] = (acc_ref[pl.ds(g0, grp), :] * jnp.concatenate([a] * (d_lat // LANES), axis=1)
                                          + pv)
            return c2

        lax.fori_loop(0, n_groups, pv_body, 0, unroll=unroll)
        return carry

    lax.fori_loop(0, nkc, chunk_body, 0)

    for h in range(n_heads):
        l = jnp.sum(l_ref[h * tq:(h + 1) * tq, :], axis=1, keepdims=True)
        o_lat = (acc_ref[h * tq:(h + 1) * tq, :] / l).astype(BF16)
        o_ref[:, h * head_dim:(h + 1) * head_dim] = jnp.dot(
            o_lat, wuv_ref[h], preferred_element_type=F32).astype(o_ref.dtype)


def _attention(zh, zs, ckv, w_uk, w_uv, bsz, seq, qi_col, q_col, n_idx, kw_col, tq, ts, slab, grp):
    t = zh.shape[0]
    nh, dl, hd = w_uv.shape
    nq = seq // tq
    k_sel = min(TOPK_MAX, seq // 4)
    assert ts >= k_sel and seq % ts == 0 and seq % tq == 0 and ts % tq == 0
    assert grp % tq == 0 and (nh * tq) % grp == 0 and tq % slab == 0 and slab % BF16_ROWS == 0
    assert tq % SCORE_ROWS == 0
    kern = functools.partial(
        _attn_kernel, tq=tq, ts=ts, seq=seq, n_idx=n_idx, n_heads=nh, d_lat=dl, head_dim=hd, k_sel=k_sel,
        idx_scale=float(IDX_DIM ** -0.5 * n_idx ** -0.5), sm_scale=float(hd ** -0.5), slab=slab, grp=grp)
    return pl.pallas_call(
        kern, grid=(bsz, nq),
        in_specs=[pl.BlockSpec((tq, n_idx * IDX_DIM), lambda b, i: (b * nq + i, qi_col)),
                  pl.BlockSpec((tq, LANES), lambda b, i: (b * nq + i, kw_col)),
                  pl.BlockSpec((seq, LANES), lambda b, i: (b, kw_col)),
                  pl.BlockSpec((seq, dl), lambda b, i: (b, 0)),
                  pl.BlockSpec((tq, nh * hd), lambda b, i: (b * nq + i, q_col)),
                  pl.BlockSpec((nh, hd, dl), lambda b, i: (0, 0, 0)),
                  pl.BlockSpec((nh, dl, hd), lambda b, i: (0, 0, 0))],
        out_specs=pl.BlockSpec((tq, nh * hd), lambda b, i: (b * nq + i, 0)),
        out_shape=jax.ShapeDtypeStruct((t, nh * hd), BF16),
        scratch_shapes=[pltpu.VMEM((tq, seq), jnp.int32),
                        pltpu.VMEM((seq, tq), jnp.int32),
                        pltpu.VMEM((tq, LANES), jnp.int32),
                        pltpu.VMEM((tq, ts), F32),
                        pltpu.VMEM((nh * tq, dl), BF16),
                        pltpu.VMEM((nh * tq, ts), F32),
                        pltpu.VMEM((nh * tq, LANES), F32),
                        pltpu.VMEM((nh * tq, LANES), F32),
                        pltpu.VMEM((nh * tq, LANES), F32),
                        pltpu.VMEM((nh * tq, LANES), F32),
                        pltpu.VMEM((nh * tq, dl), F32)],
        compiler_params=_params(2), name="dsa_attention")(zh, zs, zs, ckv, zh, w_uk, w_uv)


def _merge_kernel(ya_ref, yb_ref, yc_ref, wa_ref, wb_ref, wc_ref, g0_ref, g1_ref, g2_ref, b_ref, o_ref):
    b = b_ref[...]
    tm = o_ref.shape[0]
    sub = min(tm, MERGE_SUB_ROWS)
    for r0 in range(0, tm, sub):
        out = None
        for k, (y_ref, w_ref, g_ref) in enumerate(((ya_ref, wa_ref, g0_ref), (yb_ref, wb_ref, g1_ref),
                                                    (yc_ref, wc_ref, g2_ref))):
            gate = jax.nn.sigmoid(g_ref[r0:r0 + sub, :].astype(F32) + b[k:k + 1, :])
            term = gate * jnp.dot(y_ref[r0:r0 + sub, :], w_ref[...], preferred_element_type=F32)
            out = term if out is None else out + term
        o_ref[r0:r0 + sub, :] = out.astype(o_ref.dtype)


def _merge(ya, yb, yc, wa, wb, wc, layer, zg, b_gate, tm, tn):
    t = ya.shape[0]
    d = wa.shape[2]
    nj = d // tn

    def yspec(y):
        return pl.BlockSpec((tm, y.shape[1]), lambda i, j: (i, 0))

    def wspec(w):
        return pl.BlockSpec((None, w.shape[1], tn), lambda i, j: (layer, 0, j))

    def gspec(k):
        return pl.BlockSpec((tm, tn), lambda i, j: (i, k * nj + j))

    return pl.pallas_call(
        _merge_kernel, grid=(t // tm, nj),
        in_specs=[yspec(ya), yspec(yb), yspec(yc), wspec(wa), wspec(wb), wspec(wc),
                  gspec(0), gspec(1), gspec(2), pl.BlockSpec((b_gate.shape[0], tn), lambda i, j: (0, j))],
        out_specs=pl.BlockSpec((tm, tn), lambda i, j: (i, j)),
        out_shape=jax.ShapeDtypeStruct((t, d), BF16),
        compiler_params=_params(2), name="merge")(ya, yb, yc, wa, wb, wc, zg, zg, zg, b_gate)


def _ln_kernel(y_ref, g_ref, b_ref, o_ref, ob_ref):
    y = y_ref[...]
    mu = jnp.mean(y, axis=-1, keepdims=True)
    yc = y - mu
    var = jnp.mean(yc * yc, axis=-1, keepdims=True)
    out = yc * lax.rsqrt(var + LN_EPS) * g_ref[...] + b_ref[...]
    o_ref[...] = out
    ob_ref[...] = out.astype(BF16)


def _layer_norm(y, g, b, rows):
    t, d = y.shape
    row_spec = pl.BlockSpec((rows, d), lambda i: (i, 0))
    vec_spec = pl.BlockSpec((1, d), lambda i: (0, 0))
    return pl.pallas_call(
        _ln_kernel, grid=(t // rows,), in_specs=[row_spec, vec_spec, vec_spec],
        out_specs=[row_spec, row_spec],
        out_shape=[jax.ShapeDtypeStruct((t, d), F32), jax.ShapeDtypeStruct((t, d), BF16)],
        compiler_params=_params(1), name="layer_norm")(y, g, b)


def _ffn_up_kernel(x_ref, wg_ref, wu_ref, cw_ref, o_ref, buf_ref, carry_ref, *, tm, sub, tiles_per_seq):
    i = pl.program_id(0)
    j = pl.program_id(1)
    seq_start = i % tiles_per_seq == 0

    @pl.when(seq_start)
    def _():
        buf_ref[0:SUBLANES, :] = jnp.zeros((SUBLANES, buf_ref.shape[1]), F32)

    @pl.when(jnp.logical_not(seq_start))
    def _():
        buf_ref[0:SUBLANES, :] = carry_ref[j]

    w = cw_ref[...]
    for r0 in range(0, tm, sub):
        xs = x_ref[r0:r0 + sub, :]
        gt = jnp.dot(xs, wg_ref[...], preferred_element_type=F32)
        up = jnp.dot(xs, wu_ref[...], preferred_element_type=F32)
        buf_ref[SUBLANES + r0:SUBLANES + r0 + sub, :] = gt
        if r0 + sub == tm:
            carry_ref[j] = gt[sub - SUBLANES:sub, :]
        s1 = buf_ref[SUBLANES - 1 + r0:SUBLANES - 1 + r0 + sub, :]
        s2 = buf_ref[SUBLANES - 2 + r0:SUBLANES - 2 + r0 + sub, :]
        conv = w[0:1, :] * s2 + w[1:2, :] * s1 + w[2:3, :] * gt
        o_ref[r0:r0 + sub, :] = (jax.nn.silu(conv) * up).astype(o_ref.dtype)


def _ffn_up(xb, w_up, layer, conv_w, seq, tm, tn):
    t, d = xb.shape
    dff = conv_w.shape[1]
    assert seq % tm == 0 and dff % tn == 0 and w_up.shape[2] == 2 * dff
    nj = dff // tn
    return pl.pallas_call(
        functools.partial(_ffn_up_kernel, tm=tm, sub=min(tm, FFN_SUB_ROWS), tiles_per_seq=seq // tm),
        grid=(t // tm, nj),
        in_specs=[pl.BlockSpec((tm, d), lambda i, j: (i, 0)),
                  pl.BlockSpec((None, d, tn), lambda i, j: (layer, 0, j)),
                  pl.BlockSpec((None, d, tn), lambda i, j: (layer, 0, nj + j)),
                  pl.BlockSpec((CONV_W, tn), lambda i, j: (0, j))],
        out_specs=pl.BlockSpec((tm, tn), lambda i, j: (i, j)),
        out_shape=jax.ShapeDtypeStruct((t, dff), BF16),
        scratch_shapes=[pltpu.VMEM((SUBLANES + tm, tn), F32), pltpu.VMEM((nj, SUBLANES, tn), F32)],
        compiler_params=_params(2), name="ffn_up")(xb, w_up, w_up, conv_w)


def _tiles(seq, d_model):
    big = d_model >= 4096
    return dict(
        mm=(1024, 1024) if big else (256, 512),
        down=(512, 512) if big else (256, 512),
        ffn_up=(2048, 256) if big else (256, 256),
        merge=(1024, 512) if big else (256, 512),
        rows=512 if big else 256,
        ln_rows=512 if big else 256,
        tq=128, ts=512, slab=(32, 512))


def kernel(x, w_in, b_gate, conv_a, kv_norm, w_uk, w_uv, pool_w, pool_scale, w_br_a, w_br_b, w_br_c,
           w_o, ln1_g, ln1_b, w_up, conv_ffn_w, w_down, ln2_g, ln2_b):
    bsz, seq, d = x.shape
    depth = w_in.shape[0]
    alpha = (2.0 * depth) ** 0.25
    d_conv = conv_a.shape[2]
    n_heads, head_dim, d_lat = w_uk.shape[1:]
    d_att = n_heads * head_dim
    d_pool = pool_scale.shape[1]
    d_ff = conv_ffn_w.shape[2]
    n_in = w_in.shape[2]
    n_idx = (n_in - (3 * d_conv + d_att + d_lat + IDX_DIM + d_pool + b_gate.shape[1] * d)) // (IDX_DIM + 1)
    t = bsz * seq
    tl = _tiles(seq, d)

    o_bg, o_cg, o_v = 0, d_conv, 2 * d_conv
    o_q = 3 * d_conv
    o_ckv = o_q + d_att
    o_qi = o_ckv + d_lat
    o_ki = o_qi + n_idx * IDX_DIM
    o_wi = o_ki + IDX_DIM
    o_up = o_wi + n_idx
    o_g = o_up + d_pool
    assert o_g + b_gate.shape[1] * d == n_in
    assert IDX_DIM + n_idx <= LANES

    h_qi, h_q, h_a, h_up = 0, n_idx * IDX_DIM, n_idx * IDX_DIM + d_att, n_idx * IDX_DIM + d_att + 3 * d_conv
    n_h = h_up + d_pool
    assert h_q % d_att == 0 and h_a % d_conv == 0 and h_up % d_pool == 0 and h_qi % (n_idx * IDX_DIM) == 0
    n_s = d_lat + LANES

    wi = w_in.astype(BF16)
    w_h = jnp.concatenate([wi[:, :, o_qi:o_ki], wi[:, :, o_q:o_ckv], wi[:, :, o_bg:o_q], wi[:, :, o_up:o_g]], axis=2)
    w_s = jnp.concatenate([wi[:, :, o_ckv:o_qi], wi[:, :, o_ki:o_up],
                           jnp.zeros((depth, d, LANES - IDX_DIM - n_idx), BF16)], axis=2)
    w_g = wi[:, :, o_g:]
    wa_b, wb_b, wc_b = w_br_a.astype(BF16), w_br_b.astype(BF16), w_br_c.astype(BF16)
    wo_b, wup_b, wdn_b = w_o.astype(BF16), w_up.astype(BF16), w_down.astype(BF16)

    xf = x.reshape(t, d)
    xb = xf.astype(BF16)
    for l in range(depth):
        zh = _matmul(xb, w_h, l, BF16, *tl["mm"], name="proj_wide")
        zs = _matmul(xb, w_s, l, F32, tl["mm"][0], n_s, name="proj_small")
        zg = _matmul(xb, w_g, l, BF16, *tl["mm"], name="proj_gates")

        ya = _conv_a(zh, conv_a[l], bsz, seq, h_a // d_conv, d_conv, tl["rows"])
        yc = _pool(zh, pool_w[l].astype(BF16), pool_scale[l].reshape(1, d_pool), bsz, seq, h_up // d_pool,
                   tl["rows"])
        ckv = _ckv(zs, kv_norm[l].reshape(1, d_lat), d_lat, tl["rows"])
        yb = _attention(zh, zs, ckv, w_uk[l].astype(BF16), w_uv[l].astype(BF16), bsz, seq,
                        h_qi // (n_idx * IDX_DIM), h_q // d_att, n_idx, d_lat // LANES,
                        tl["tq"], min(tl["ts"], seq), *tl["slab"])

        merged = _merge(ya, yb, yc, wa_b, wb_b, wc_b, l, zg, b_gate[l], *tl["merge"])
        y1 = _matmul(merged, wo_b, l, F32, *tl["mm"], res=xf, alpha=alpha, name="proj_out")
        xf, xb = _layer_norm(y1, ln1_g[l].reshape(1, d), ln1_b[l].reshape(1, d), tl["ln_rows"])

        h = _ffn_up(xb, wup_b, l, conv_ffn_w[l], seq, *tl["ffn_up"])
        y2 = _matmul(h, wdn_b, l, F32, *tl["down"], res=xf, alpha=alpha, name="ffn_down")
        xf, xb = _layer_norm(y2, ln2_g[l].reshape(1, d), ln2_b[l].reshape(1, d), tl["ln_rows"])
    return xf.reshape(bsz, seq, d)
```

```python
import functools

import jax
import jax.numpy as jnp
from jax import lax
from jax.experimental import pallas as pl
from jax.experimental.pallas import tpu as pltpu

IDX_DIM = 64
TOPK_MAX = 256
POOL_WINDOWS = (2, 4, 8, 16)
CONV_W = 3
LN_EPS = 1e-5
RMS_EPS = 1e-6

LANES = 128
SUBLANES = 8
BF16_ROWS = 16
VMEM_BYTES_V7X = 64 * 1024 * 1024
VMEM_LIMIT = VMEM_BYTES_V7X - 8 * 1024 * 1024

HALO = BF16_ROWS
SCORE_ROWS = 128
EARLY_EXIT_BIT = 22
COUNT_ROWS = 32
GROUP_UNROLL = 4
MERGE_SUB_ROWS = 256
FFN_SUB_ROWS = 256
NEG_BIG = -1e30
LOG2E = 1.4426950408889634
F32 = jnp.float32
BF16 = jnp.bfloat16


def _params(n_axes):
    return pltpu.CompilerParams(dimension_semantics=("arbitrary",) * n_axes,
                                vmem_limit_bytes=VMEM_LIMIT)


def _mm_kernel(x_ref, w_ref, o_ref):
    o_ref[...] = jnp.dot(x_ref[...], w_ref[...], preferred_element_type=F32).astype(o_ref.dtype)


def _mm_res_kernel(x_ref, w_ref, r_ref, o_ref, *, alpha):
    acc = jnp.dot(x_ref[...], w_ref[...], preferred_element_type=F32)
    o_ref[...] = (alpha * r_ref[...] + acc).astype(o_ref.dtype)


def _matmul(x, w, layer, out_dtype, tm, tn, res=None, alpha=None, name="mm"):
    m, k = x.shape
    n = w.shape[2]
    tm, tn = min(tm, m), min(tn, n)
    assert m % tm == 0 and n % tn == 0, (m, n, tm, tn)
    in_specs = [pl.BlockSpec((tm, k), lambda i, j: (i, 0)),
                pl.BlockSpec((None, k, tn), lambda i, j: (layer, 0, j))]
    args = [x, w]
    kern = _mm_kernel
    if res is not None:
        in_specs.append(pl.BlockSpec((tm, tn), lambda i, j: (i, j)))
        args.append(res)
        kern = functools.partial(_mm_res_kernel, alpha=alpha)
    return pl.pallas_call(
        kern, grid=(m // tm, n // tn), in_specs=in_specs,
        out_specs=pl.BlockSpec((tm, tn), lambda i, j: (i, j)),
        out_shape=jax.ShapeDtypeStruct((m, n), out_dtype),
        compiler_params=_params(2), name=name)(*args)


def _conv_a_kernel(bg_ref, cg_ref, v_ref, cgp_ref, vp_ref, w_ref, o_ref, buf_ref, *, rows):
    i = pl.program_id(1)
    u = cg_ref[...].astype(F32) * v_ref[...].astype(F32)
    up = cgp_ref[...].astype(F32) * vp_ref[...].astype(F32)
    buf_ref[0:HALO, :] = jnp.where(i > 0, up, 0.0)
    buf_ref[HALO:HALO + rows, :] = u
    s1 = buf_ref[HALO - 1:HALO - 1 + rows, :]
    s2 = buf_ref[HALO - 2:HALO - 2 + rows, :]
    w = w_ref[...]
    conv = w[0:1, :] * s2 + w[1:2, :] * s1 + w[2:3, :] * u
    o_ref[...] = (bg_ref[...].astype(F32) * conv).astype(o_ref.dtype)


def _conv_a(zh, conv_w, bsz, seq, col0, width, rows):
    t = zh.shape[0]
    nb = seq // rows
    hb = rows // HALO

    def cur(c):
        return pl.BlockSpec((rows, width), lambda b, i: (b * nb + i, col0 + c))

    def prev(c):
        return pl.BlockSpec((HALO, width), lambda b, i: (jnp.maximum((b * nb + i) * hb - 1, 0), col0 + c))

    return pl.pallas_call(
        functools.partial(_conv_a_kernel, rows=rows), grid=(bsz, nb),
        in_specs=[cur(0), cur(1), cur(2), prev(1), prev(2),
                  pl.BlockSpec((CONV_W, width), lambda b, i: (0, 0))],
        out_specs=pl.BlockSpec((rows, width), lambda b, i: (b * nb + i, 0)),
        out_shape=jax.ShapeDtypeStruct((t, width), BF16),
        scratch_shapes=[pltpu.VMEM((HALO + rows, width), F32)],
        compiler_params=_params(2), name="conv_a")(zh, zh, zh, zh, zh, conv_w)


def _pool_kernel(u_ref, up_ref, pw_ref, ps_ref, o_ref, buf_ref, *, rows, group):
    i = pl.program_id(1)
    ext = HALO + rows
    pos = i * rows + lax.broadcasted_iota(jnp.int32, (rows, 1), 0)
    buf_ref[0:HALO, :] = jnp.zeros((HALO, group), F32)
    for g, win in enumerate(POOL_WINDOWS):
        sl = slice(g * group, (g + 1) * group)
        u = u_ref[:, sl].astype(F32)
        buf_ref[HALO:2 * HALO, :] = jnp.where(i > 0, up_ref[:, sl].astype(F32), 0.0)
        buf_ref[2 * HALO:2 * HALO + rows, :] = u
        k = 1
        while k < win:
            s = buf_ref[HALO:HALO + ext, :] + buf_ref[HALO - k:HALO - k + ext, :]
            buf_ref[HALO:HALO + ext, :] = s
            k *= 2
        wsum = buf_ref[2 * HALO:2 * HALO + rows, :]
        cnt = jnp.minimum(pos + 1, win).astype(F32)
        d = wsum / cnt - u
        y = jnp.dot(d.astype(BF16), pw_ref[g], preferred_element_type=F32)
        o_ref[:, sl] = (y * ps_ref[:, sl]).astype(o_ref.dtype)


def _pool(zh, pool_w, pool_scale, bsz, seq, col0, rows):
    t = zh.shape[0]
    ngroups, group, _ = pool_w.shape
    width = ngroups * group
    nb = seq // rows
    hb = rows // HALO
    return pl.pallas_call(
        functools.partial(_pool_kernel, rows=rows, group=group), grid=(bsz, nb),
        in_specs=[pl.BlockSpec((rows, width), lambda b, i: (b * nb + i, col0)),
                  pl.BlockSpec((HALO, width), lambda b, i: (jnp.maximum((b * nb + i) * hb - 1, 0), col0)),
                  pl.BlockSpec((ngroups, group, group), lambda b, i: (0, 0, 0)),
                  pl.BlockSpec((1, width), lambda b, i: (0, 0))],
        out_specs=pl.BlockSpec((rows, width), lambda b, i: (b * nb + i, 0)),
        out_shape=jax.ShapeDtypeStruct((t, width), BF16),
        scratch_shapes=[pltpu.VMEM((2 * HALO + rows, group), F32)],
        compiler_params=_params(2), name="pool")(zh, zh, pool_w, pool_scale)


def _ckv_kernel(z_ref, g_ref, o_ref, *, d_lat):
    c = z_ref[:, 0:d_lat]
    ms = jnp.mean(c * c, axis=-1, keepdims=True)
    o_ref[...] = (c * lax.rsqrt(ms + RMS_EPS) * g_ref[...]).astype(o_ref.dtype)


def _ckv(zs, kv_norm, d_lat, rows):
    t, ns = zs.shape
    return pl.pallas_call(
        functools.partial(_ckv_kernel, d_lat=d_lat), grid=(t // rows,),
        in_specs=[pl.BlockSpec((rows, ns), lambda i: (i, 0)),
                  pl.BlockSpec((1, d_lat), lambda i: (0, 0))],
        out_specs=pl.BlockSpec((rows, d_lat), lambda i: (i, 0)),
        out_shape=jax.ShapeDtypeStruct((t, d_lat), BF16),
        compiler_params=_params(1), name="ckv_norm")(zs, kv_norm)


def _count_ge(key_ref, nkc, ts, tq, pred):
    def body(kc, part):
        m = pred(key_ref[:, pl.ds(pl.multiple_of(kc * ts, ts), ts)], kc).astype(F32)
        for c in range(ts // LANES):
            part = part + m[:, c * LANES:(c + 1) * LANES]
        return part

    part = lax.fori_loop(0, nkc, body, jnp.zeros((tq, LANES), F32))
    return jnp.sum(part, axis=1, keepdims=True)


def _kth_largest_key(keyt_ref, nkc, ts, tq, k, few_keys):
    sign = jnp.int32(-2 ** 31)

    def count_ge(cand):
        def body(kc, part):
            m = (keyt_ref[pl.ds(pl.multiple_of(kc * ts, ts), ts), :] >= cand).astype(F32)
            for r in range(ts // COUNT_ROWS):
                part = part + m[r * COUNT_ROWS:(r + 1) * COUNT_ROWS, :]
            return part

        part = lax.fori_loop(0, nkc, body, jnp.zeros((COUNT_ROWS, tq), F32))
        return jnp.sum(part, axis=0, keepdims=True)

    def body(b, carry):
        v, n_v = carry
        cand = v | lax.shift_left(jnp.int32(1), jnp.int32(31) - b)
        n_cand = count_ge(cand ^ sign)
        keep = n_cand >= k
        return jnp.where(keep, cand, v), jnp.where(keep, n_cand, n_v)

    n_all = jnp.broadcast_to((nkc * ts).astype(F32), (1, tq))
    carry = lax.fori_loop(0, EARLY_EXIT_BIT, body, (jnp.zeros((1, tq), jnp.int32), n_all))

    def unresolved(carry):
        _, n_v = carry
        return jnp.max(jnp.where((n_v == k) | few_keys, 0.0, 1.0)) > 0.0

    def cond(state):
        b, go, _ = state
        return jnp.logical_and(b < 32, go)

    def step(state):
        b, _, carry = state
        carry = body(b + 1, body(b, carry))
        return b + 2, unresolved(carry), carry

    _, _, (v, n_v) = lax.while_loop(cond, step, (jnp.int32(EARLY_EXIT_BIT), unresolved(carry), carry))
    return v ^ sign, n_v


def _bisect_largest(nbits, target, count_fn, tq):
    sign = jnp.int32(-2 ** 31) if nbits == 32 else jnp.int32(0)

    def body(b, v):
        bit = lax.shift_left(jnp.int32(1), jnp.int32(nbits - 1) - b)
        cand = v | bit
        cnt = count_fn(cand ^ sign)
        return jnp.where(cnt >= target, cand, v)

    v = lax.fori_loop(0, nbits, body, jnp.zeros((tq, 1), jnp.int32))
    return v ^ sign


def _attn_kernel(qi_ref, zq_ref, zk_ref, ckv_ref, q_ref, wuk_ref, wuv_ref, o_ref,
                 key_ref, keyt_ref, cut_ref, bias_ref, qs_ref, sc_ref, cmax_ref, m_ref, l_ref, a_ref, acc_ref, *, tq, ts,
                 seq, n_idx,
                 n_heads, d_lat, head_dim, k_sel, idx_scale, sm_scale, slab, grp):
    i = pl.program_id(1)
    q0 = i * tq
    nkc = (q0 + tq + ts - 1) // ts
    nbits_idx = max(1, (seq - 1).bit_length())
    row = q0 + lax.broadcasted_iota(jnp.int32, (tq, 1), 0)
    int_min = jnp.int32(-2 ** 31)

    wq = zq_ref[:, IDX_DIM:IDX_DIM + n_idx] * idx_scale

    def score_body(kc, carry):
        s0 = pl.multiple_of(kc * ts, ts)
        ki = zk_ref[pl.ds(s0, ts), 0:IDX_DIM].astype(BF16)
        col = s0 + lax.broadcasted_iota(jnp.int32, (SCORE_ROWS, ts), 1)
        for r0 in range(0, tq, SCORE_ROWS):
            acc = jnp.zeros((SCORE_ROWS, ts), F32)
            for h in range(n_idx):
                lg = lax.dot_general(qi_ref[r0:r0 + SCORE_ROWS, h * IDX_DIM:(h + 1) * IDX_DIM], ki,
                                     (((1,), (1,)), ((), ())), preferred_element_type=F32)
                acc = acc + wq[r0:r0 + SCORE_ROWS, h:h + 1] * jnp.maximum(lg, 0.0)
            bits = lax.bitcast_convert_type(acc, jnp.int32)
            key = bits ^ (lax.shift_right_arithmetic(bits, 31) & jnp.int32(0x7FFFFFFF))
            key = jnp.where(col <= row[r0:r0 + SCORE_ROWS], key, int_min)
            key_ref[r0:r0 + SCORE_ROWS, pl.ds(s0, ts)] = key
            keyt_ref[pl.ds(s0, ts), r0:r0 + SCORE_ROWS] = key.T
        return carry

    lax.fori_loop(0, nkc, score_body, 0)

    kf = jnp.float32(k_sel)
    few_keys = (q0 + lax.broadcasted_iota(jnp.int32, (1, tq), 1)) + 1 < k_sel
    thr_t, n_ge = _kth_largest_key(keyt_ref, nkc, ts, tq, kf, few_keys)
    thr = jnp.broadcast_to(thr_t, (SUBLANES, tq)).T[:, 0:1]

    def rev_idx(kc):
        return (seq - 1) - (kc * ts + lax.broadcasted_iota(jnp.int32, (tq, ts), 1))

    cut_ref[...] = jnp.zeros((tq, LANES), jnp.int32)

    @pl.when(jnp.max(jnp.where(thr_t > int_min, n_ge, 0.0)) > kf)
    def _():
        n_gt = _count_ge(key_ref, nkc, ts, tq, lambda x, kc: x > thr)
        need = kf - n_gt
        c = _bisect_largest(
            nbits_idx, need,
            lambda v: _count_ge(key_ref, nkc, ts, tq, lambda x, kc: (x == thr) & (rev_idx(kc) >= v)), tq)
        cut_ref[...] = jnp.broadcast_to(c, (tq, LANES))

    cut = cut_ref[:, 0:1]

    rows_all = n_heads * tq
    nlc = ts // LANES
    k2 = sm_scale * LOG2E
    m_ref[...] = jnp.full((rows_all, LANES), NEG_BIG, F32)
    l_ref[...] = jnp.zeros((rows_all, LANES), F32)
    acc_ref[...] = jnp.zeros((rows_all, d_lat), F32)
    for h in range(n_heads):
        qs_ref[h * tq:(h + 1) * tq, :] = jnp.dot(q_ref[:, h * head_dim:(h + 1) * head_dim], wuk_ref[h],
                                                 preferred_element_type=F32).astype(BF16)
    n_groups = rows_all // grp
    unroll = min(GROUP_UNROLL, n_groups)

    def chunk_body(kc, carry):
        s0 = pl.multiple_of(kc * ts, ts)
        x = key_ref[:, pl.ds(s0, ts)]
        col = s0 + lax.broadcasted_iota(jnp.int32, (tq, ts), 1)
        sel = ((x > thr) | ((x == thr) & (rev_idx(kc) >= cut))) & (col <= row)
        bias_ref[...] = jnp.where(sel, 0.0, NEG_BIG)
        ckv = ckv_ref[pl.ds(s0, ts), :]

        def qk_body(g, c2):
            g0 = pl.multiple_of(g * grp, grp)
            sc = lax.dot_general(qs_ref[pl.ds(g0, grp), :], ckv, (((1,), (1,)), ((), ())),
                                 preferred_element_type=F32)
            for s in range(grp // slab):
                r0, b0 = s * slab, (s * slab) % tq
                rows_s = pl.ds(g0 + r0, slab)
                xs = [sc[r0:r0 + slab, c * LANES:(c + 1) * LANES] * k2
                      + bias_ref[b0:b0 + slab, c * LANES:(c + 1) * LANES] for c in range(nlc)]
                for c in range(nlc):
                    sc_ref[rows_s, c * LANES:(c + 1) * LANES] = xs[c]
                cmax_ref[rows_s, :] = functools.reduce(jnp.maximum, xs)
            return c2

        lax.fori_loop(0, n_groups, qk_body, 0, unroll=unroll)

        m_old = m_ref[...]
        m_new = jnp.maximum(m_old, jnp.max(cmax_ref[...], axis=1, keepdims=True))
        a_ref[...] = jnp.exp2(m_old - m_new)
        m_ref[...] = m_new

        def pv_body(g, c2):
            g0 = pl.multiple_of(g * grp, grp)
            probs = []
            for s in range(grp // slab):
                rows_s = pl.ds(g0 + s * slab, slab)
                m = m_ref[rows_s, :]
                ps = [jnp.exp2(sc_ref[rows_s, c * LANES:(c + 1) * LANES] - m) for c in range(nlc)]
                l_ref[rows_s, :] = a_ref[rows_s, :] * l_ref[rows_s, :] + functools.reduce(lambda a, b: a + b, ps)
                probs.append(jnp.concatenate([pc.astype(BF16) for pc in ps], axis=1))
            pv = jnp.dot(jnp.concatenate(probs, axis=0), ckv, preferred_element_type=F32)
            a = a_ref[pl.ds(g0, grp), :]
            acc_ref[pl.ds(g0, grp), :] = (acc_ref[pl.ds(g0, grp), :] * jnp.concatenate([a] * (d_lat // LANES), axis=1)
                                          + pv)
            return c2

        lax.fori_loop(0, n_groups, pv_body, 0, unroll=unroll)
        return carry

    lax.fori_loop(0, nkc, chunk_body, 0)

    for h in range(n_heads):
        l = jnp.sum(l_ref[h * tq:(h + 1) * tq, :], axis=1, keepdims=True)
        o_lat = (acc_ref[h * tq:(h + 1) * tq, :] / l).astype(BF16)
        o_ref[:, h * head_dim:(h + 1) * head_dim] = jnp.dot(
            o_lat, wuv_ref[h], preferred_element_type=F32).astype(o_ref.dtype)


def _attention(zh, zs, ckv, w_uk, w_uv, bsz, seq, qi_col, q_col, n_idx, kw_col, tq, ts, slab, grp):
    t = zh.shape[0]
    nh, dl, hd = w_uv.shape
    nq = seq // tq
    k_sel = min(TOPK_MAX, seq // 4)
    assert ts >= k_sel and seq % ts == 0 and seq % tq == 0 and ts % tq == 0
    assert grp % tq == 0 and (nh * tq) % grp == 0 and tq % slab == 0 and slab % BF16_ROWS == 0
    assert tq % SCORE_ROWS == 0
    kern = functools.partial(
        _attn_kernel, tq=tq, ts=ts, seq=seq, n_idx=n_idx, n_heads=nh, d_lat=dl, head_dim=hd, k_sel=k_sel,
        idx_scale=float(IDX_DIM ** -0.5 * n_idx ** -0.5), sm_scale=float(hd ** -0.5), slab=slab, grp=grp)
    return pl.pallas_call(
        kern, grid=(bsz, nq),
        in_specs=[pl.BlockSpec((tq, n_idx * IDX_DIM), lambda b, i: (b * nq + i, qi_col)),
                  pl.BlockSpec((tq, LANES), lambda b, i: (b * nq + i, kw_col)),
                  pl.BlockSpec((seq, LANES), lambda b, i: (b, kw_col)),
                  pl.BlockSpec((seq, dl), lambda b, i: (b, 0)),
                  pl.BlockSpec((tq, nh * hd), lambda b, i: (b * nq + i, q_col)),
                  pl.BlockSpec((nh, hd, dl), lambda b, i: (0, 0, 0)),
                  pl.BlockSpec((nh, dl, hd), lambda b, i: (0, 0, 0))],
        out_specs=pl.BlockSpec((tq, nh * hd), lambda b, i: (b * nq + i, 0)),
        out_shape=jax.ShapeDtypeStruct((t, nh * hd), BF16),
        scratch_shapes=[pltpu.VMEM((tq, seq), jnp.int32),
                        pltpu.VMEM((seq, tq), jnp.int32),
                        pltpu.VMEM((tq, LANES), jnp.int32),
                        pltpu.VMEM((tq, ts), F32),
                        pltpu.VMEM((nh * tq, dl), BF16),
                        pltpu.VMEM((nh * tq, ts), F32),
                        pltpu.VMEM((nh * tq, LANES), F32),
                        pltpu.VMEM((nh * tq, LANES), F32),
                        pltpu.VMEM((nh * tq, LANES), F32),
                        pltpu.VMEM((nh * tq, LANES), F32),
                        pltpu.VMEM((nh * tq, dl), F32)],
        compiler_params=_params(2), name="dsa_attention")(zh, zs, zs, ckv, zh, w_uk, w_uv)


def _merge_kernel(ya_ref, yb_ref, yc_ref, wa_ref, wb_ref, wc_ref, g0_ref, g1_ref, g2_ref, b_ref, o_ref):
    b = b_ref[...]
    tm = o_ref.shape[0]
    sub = min(tm, MERGE_SUB_ROWS)
    for r0 in range(0, tm, sub):
        out = None
        for k, (y_ref, w_ref, g_ref) in enumerate(((ya_ref, wa_ref, g0_ref), (yb_ref, wb_ref, g1_ref),
                                                    (yc_ref, wc_ref, g2_ref))):
            gate = jax.nn.sigmoid(g_ref[r0:r0 + sub, :].astype(F32) + b[k:k + 1, :])
            term = gate * jnp.dot(y_ref[r0:r0 + sub, :], w_ref[...], preferred_element_type=F32)
            out = term if out is None else out + term
        o_ref[r0:r0 + sub, :] = out.astype(o_ref.dtype)


def _merge(ya, yb, yc, wa, wb, wc, layer, zg, b_gate, tm, tn):
    t = ya.shape[0]
    d = wa.shape[2]
    nj = d // tn

    def yspec(y):
        return pl.BlockSpec((tm, y.shape[1]), lambda i, j: (i, 0))

    def wspec(w):
        return pl.BlockSpec((None, w.shape[1], tn), lambda i, j: (layer, 0, j))

    def gspec(k):
        return pl.BlockSpec((tm, tn), lambda i, j: (i, k * nj + j))

    return pl.pallas_call(
        _merge_kernel, grid=(t // tm, nj),
        in_specs=[yspec(ya), yspec(yb), yspec(yc), wspec(wa), wspec(wb), wspec(wc),
                  gspec(0), gspec(1), gspec(2), pl.BlockSpec((b_gate.shape[0], tn), lambda i, j: (0, j))],
        out_specs=pl.BlockSpec((tm, tn), lambda i, j: (i, j)),
        out_shape=jax.ShapeDtypeStruct((t, d), BF16),
        compiler_params=_params(2), name="merge")(ya, yb, yc, wa, wb, wc, zg, zg, zg, b_gate)


def _ln_kernel(y_ref, g_ref, b_ref, o_ref, ob_ref):
    y = y_ref[...]
    mu = jnp.mean(y, axis=-1, keepdims=True)
    yc = y - mu
    var = jnp.mean(yc * yc, axis=-1, keepdims=True)
    out = yc * lax.rsqrt(var + LN_EPS) * g_ref[...] + b_ref[...]
    o_ref[...] = out
    ob_ref[...] = out.astype(BF16)


def _layer_norm(y, g, b, rows):
    t, d = y.shape
    row_spec = pl.BlockSpec((rows, d), lambda i: (i, 0))
    vec_spec = pl.BlockSpec((1, d), lambda i: (0, 0))
    return pl.pallas_call(
        _ln_kernel, grid=(t // rows,), in_specs=[row_spec, vec_spec, vec_spec],
        out_specs=[row_spec, row_spec],
        out_shape=[jax.ShapeDtypeStruct((t, d), F32), jax.ShapeDtypeStruct((t, d), BF16)],
        compiler_params=_params(1), name="layer_norm")(y, g, b)


def _ffn_up_kernel(x_ref, wg_ref, wu_ref, cw_ref, o_ref, buf_ref, carry_ref, *, tm, sub, tiles_per_seq):
    i = pl.program_id(0)
    j = pl.program_id(1)
    seq_start = i % tiles_per_seq == 0

    @pl.when(seq_start)
    def _():
        buf_ref[0:SUBLANES, :] = jnp.zeros((SUBLANES, buf_ref.shape[1]), F32)

    @pl.when(jnp.logical_not(seq_start))
    def _():
        buf_ref[0:SUBLANES, :] = carry_ref[j]

    w = cw_ref[...]
    for r0 in range(0, tm, sub):
        xs = x_ref[r0:r0 + sub, :]
        gt = jnp.dot(xs, wg_ref[...], preferred_element_type=F32)
        up = jnp.dot(xs, wu_ref[...], preferred_element_type=F32)
        buf_ref[SUBLANES + r0:SUBLANES + r0 + sub, :] = gt
        if r0 + sub == tm:
            carry_ref[j] = gt[sub - SUBLANES:sub, :]
        s1 = buf_ref[SUBLANES - 1 + r0:SUBLANES - 1 + r0 + sub, :]
        s2 = buf_ref[SUBLANES - 2 + r0:SUBLANES - 2 + r0 + sub, :]
        conv = w[0:1, :] * s2 + w[1:2, :] * s1 + w[2:3, :] * gt
        o_ref[r0:r0 + sub, :] = (jax.nn.silu(conv) * up).astype(o_ref.dtype)


def _ffn_up(xb, w_up, layer, conv_w, seq, tm, tn):
    t, d = xb.shape
    dff = conv_w.shape[1]
    assert seq % tm == 0 and dff % tn == 0 and w_up.shape[2] == 2 * dff
    nj = dff // tn
    return pl.pallas_call(
        functools.partial(_ffn_up_kernel, tm=tm, sub=min(tm, FFN_SUB_ROWS), tiles_per_seq=seq // tm),
        grid=(t // tm, nj),
        in_specs=[pl.BlockSpec((tm, d), lambda i, j: (i, 0)),
                  pl.BlockSpec((None, d, tn), lambda i, j: (layer, 0, j)),
                  pl.BlockSpec((None, d, tn), lambda i, j: (layer, 0, nj + j)),
                  pl.BlockSpec((CONV_W, tn), lambda i, j: (0, j))],
        out_specs=pl.BlockSpec((tm, tn), lambda i, j: (i, j)),
        out_shape=jax.ShapeDtypeStruct((t, dff), BF16),
        scratch_shapes=[pltpu.VMEM((SUBLANES + tm, tn), F32), pltpu.VMEM((nj, SUBLANES, tn), F32)],
        compiler_params=_params(2), name="ffn_up")(xb, w_up, w_up, conv_w)


def _tiles(seq, d_model):
    big = d_model >= 4096
    return dict(
        mm=(1024, 1024) if big else (256, 512),
        down=(512, 512) if big else (256, 512),
        ffn_up=(2048, 256) if big else (256, 256),
        merge=(1024, 512) if big else (256, 512),
        rows=512 if big else 256,
        ln_rows=512 if big else 256,
        tq=128, ts=512, slab=(32, 512))


def kernel(x, w_in, b_gate, conv_a, kv_norm, w_uk, w_uv, pool_w, pool_scale, w_br_a, w_br_b, w_br_c,
           w_o, ln1_g, ln1_b, w_up, conv_ffn_w, w_down, ln2_g, ln2_b):
    bsz, seq, d = x.shape
    depth = w_in.shape[0]
    alpha = (2.0 * depth) ** 0.25
    d_conv = conv_a.shape[2]
    n_heads, head_dim, d_lat = w_uk.shape[1:]
    d_att = n_heads * head_dim
    d_pool = pool_scale.shape[1]
    d_ff = conv_ffn_w.shape[2]
    n_in = w_in.shape[2]
    n_idx = (n_in - (3 * d_conv + d_att + d_lat + IDX_DIM + d_pool + b_gate.shape[1] * d)) // (IDX_DIM + 1)
    t = bsz * seq
    tl = _tiles(seq, d)

    o_bg, o_cg, o_v = 0, d_conv, 2 * d_conv
    o_q = 3 * d_conv
    o_ckv = o_q + d_att
    o_qi = o_ckv + d_lat
    o_ki = o_qi + n_idx * IDX_DIM
    o_wi = o_ki + IDX_DIM
    o_up = o_wi + n_idx
    o_g = o_up + d_pool
    assert o_g + b_gate.shape[1] * d == n_in
    assert IDX_DIM + n_idx <= LANES

    h_qi, h_q, h_a, h_up = 0, n_idx * IDX_DIM, n_idx * IDX_DIM + d_att, n_idx * IDX_DIM + d_att + 3 * d_conv
    n_h = h_up + d_pool
    assert h_q % d_att == 0 and h_a % d_conv == 0 and h_up % d_pool == 0 and h_qi % (n_idx * IDX_DIM) == 0
    n_s = d_lat + LANES

    wi = w_in.astype(BF16)
    w_h = jnp.concatenate([wi[:, :, o_qi:o_ki], wi[:, :, o_q:o_ckv], wi[:, :, o_bg:o_q], wi[:, :, o_up:o_g]], axis=2)
    w_s = jnp.concatenate([wi[:, :, o_ckv:o_qi], wi[:, :, o_ki:o_up],
                           jnp.zeros((depth, d, LANES - IDX_DIM - n_idx), BF16)], axis=2)
    w_g = wi[:, :, o_g:]
    wa_b, wb_b, wc_b = w_br_a.astype(BF16), w_br_b.astype(BF16), w_br_c.astype(BF16)
    wo_b, wup_b, wdn_b = w_o.astype(BF16), w_up.astype(BF16), w_down.astype(BF16)

    xf = x.reshape(t, d)
    xb = xf.astype(BF16)
    for l in range(depth):
        zh = _matmul(xb, w_h, l, BF16, *tl["mm"], name="proj_wide")
        zs = _matmul(xb, w_s, l, F32, tl["mm"][0], n_s, name="proj_small")
        zg = _matmul(xb, w_g, l, BF16, *tl["mm"], name="proj_gates")

        ya = _conv_a(zh, conv_a[l], bsz, seq, h_a // d_conv, d_conv, tl["rows"])
        yc = _pool(zh, pool_w[l].astype(BF16), pool_scale[l].reshape(1, d_pool), bsz, seq, h_up // d_pool,
                   tl["rows"])
        ckv = _ckv(zs, kv_norm[l].reshape(1, d_lat), d_lat, tl["rows"])
        yb = _attention(zh, zs, ckv, w_uk[l].astype(BF16), w_uv[l].astype(BF16), bsz, seq,
                        h_qi // (n_idx * IDX_DIM), h_q // d_att, n_idx, d_lat // LANES,
                        tl["tq"], min(tl["ts"], seq), *tl["slab"])

        merged = _merge(ya, yb, yc, wa_b, wb_b, wc_b, l, zg, b_gate[l], *tl["merge"])
        y1 = _matmul(merged, wo_b, l, F32, *tl["mm"], res=xf, alpha=alpha, name="proj_out")
        xf, xb = _layer_norm(y1, ln1_g[l].reshape(1, d), ln1_b[l].reshape(1, d), tl["ln_rows"])

        h = _ffn_up(xb, wup_b, l, conv_ffn_w[l], seq, *tl["ffn_up"])
        y2 = _matmul(h, wdn_b, l, F32, *tl["down"], res=xf, alpha=alpha, name="ffn_down")
        xf, xb = _layer_norm(y2, ln2_g[l].reshape(1, d), ln2_b[l].reshape(1, d), tl["ln_rows"])
    return xf.reshape(bsz, seq, d)
```

```python
import functools

import jax
import jax.numpy as jnp
from jax import lax
from jax.experimental import pallas as pl
from jax.experimental.pallas import tpu as pltpu

IDX_DIM = 64
TOPK_MAX = 256
POOL_WINDOWS = (2, 4, 8, 16)
CONV_W = 3
LN_EPS = 1e-5
RMS_EPS = 1e-6

LANES = 128
SUBLANES = 8
BF16_ROWS = 16
VMEM_BYTES_V7X = 64 * 1024 * 1024
VMEM_LIMIT = VMEM_BYTES_V7X - 8 * 1024 * 1024

HALO = BF16_ROWS
SCORE_ROWS = 128
EARLY_EXIT_BIT = 22
COUNT_ROWS = 32
GROUP_UNROLL = 4
MERGE_SUB_ROWS = 256
FFN_SUB_ROWS = 256
NEG_BIG = -1e30
LOG2E = 1.4426950408889634
F32 = jnp.float32
BF16 = jnp.bfloat16


def _params(n_axes):
    return pltpu.CompilerParams(dimension_semantics=("arbitrary",) * n_axes,
                                vmem_limit_bytes=VMEM_LIMIT)


def _mm_kernel(x_ref, w_ref, o_ref):
    o_ref[...] = jnp.dot(x_ref[...], w_ref[...], preferred_element_type=F32).astype(o_ref.dtype)


def _mm_res_kernel(x_ref, w_ref, r_ref, o_ref, *, alpha):
    acc = jnp.dot(x_ref[...], w_ref[...], preferred_element_type=F32)
    o_ref[...] = (alpha * r_ref[...] + acc).astype(o_ref.dtype)


def _matmul(x, w, layer, out_dtype, tm, tn, res=None, alpha=None, name="mm"):
    m, k = x.shape
    n = w.shape[2]
    tm, tn = min(tm, m), min(tn, n)
    assert m % tm == 0 and n % tn == 0, (m, n, tm, tn)
    in_specs = [pl.BlockSpec((tm, k), lambda i, j: (i, 0)),
                pl.BlockSpec((None, k, tn), lambda i, j: (layer, 0, j))]
    args = [x, w]
    kern = _mm_kernel
    if res is not None:
        in_specs.append(pl.BlockSpec((tm, tn), lambda i, j: (i, j)))
        args.append(res)
        kern = functools.partial(_mm_res_kernel, alpha=alpha)
    return pl.pallas_call(
        kern, grid=(m // tm, n // tn), in_specs=in_specs,
        out_specs=pl.BlockSpec((tm, tn), lambda i, j: (i, j)),
        out_shape=jax.ShapeDtypeStruct((m, n), out_dtype),
        compiler_params=_params(2), name=name)(*args)


def _conv_a_kernel(bg_ref, cg_ref, v_ref, cgp_ref, vp_ref, w_ref, o_ref, buf_ref, *, rows):
    i = pl.program_id(1)
    u = cg_ref[...].astype(F32) * v_ref[...].astype(F32)
    up = cgp_ref[...].astype(F32) * vp_ref[...].astype(F32)
    buf_ref[0:HALO, :] = jnp.where(i > 0, up, 0.0)
    buf_ref[HALO:HALO + rows, :] = u
    s1 = buf_ref[HALO - 1:HALO - 1 + rows, :]
    s2 = buf_ref[HALO - 2:HALO - 2 + rows, :]
    w = w_ref[...]
    conv = w[0:1, :] * s2 + w[1:2, :] * s1 + w[2:3, :] * u
    o_ref[...] = (bg_ref[...].astype(F32) * conv).astype(o_ref.dtype)


def _conv_a(zh, conv_w, bsz, seq, col0, width, rows):
    t = zh.shape[0]
    nb = seq // rows
    hb = rows // HALO

    def cur(c):
        return pl.BlockSpec((rows, width), lambda b, i: (b * nb + i, col0 + c))

    def prev(c):
        return pl.BlockSpec((HALO, width), lambda b, i: (jnp.maximum((b * nb + i) * hb - 1, 0), col0 + c))

    return pl.pallas_call(
        functools.partial(_conv_a_kernel, rows=rows), grid=(bsz, nb),
        in_specs=[cur(0), cur(1), cur(2), prev(1), prev(2),
                  pl.BlockSpec((CONV_W, width), lambda b, i: (0, 0))],
        out_specs=pl.BlockSpec((rows, width), lambda b, i: (b * nb + i, 0)),
        out_shape=jax.ShapeDtypeStruct((t, width), BF16),
        scratch_shapes=[pltpu.VMEM((HALO + rows, width), F32)],
        compiler_params=_params(2), name="conv_a")(zh, zh, zh, zh, zh, conv_w)


def _pool_kernel(u_ref, up_ref, pw_ref, ps_ref, o_ref, buf_ref, *, rows, group):
    i = pl.program_id(1)
    ext = HALO + rows
    pos = i * rows + lax.broadcasted_iota(jnp.int32, (rows, 1), 0)
    buf_ref[0:HALO, :] = jnp.zeros((HALO, group), F32)
    for g, win in enumerate(POOL_WINDOWS):
        sl = slice(g * group, (g + 1) * group)
        u = u_ref[:, sl].astype(F32)
        buf_ref[HALO:2 * HALO, :] = jnp.where(i > 0, up_ref[:, sl].astype(F32), 0.0)
        buf_ref[2 * HALO:2 * HALO + rows, :] = u
        k = 1
        while k < win:
            s = buf_ref[HALO:HALO + ext, :] + buf_ref[HALO - k:HALO - k + ext, :]
            buf_ref[HALO:HALO + ext, :] = s
            k *= 2
        wsum = buf_ref[2 * HALO:2 * HALO + rows, :]
        cnt = jnp.minimum(pos + 1, win).astype(F32)
        d = wsum / cnt - u
        y = jnp.dot(d.astype(BF16), pw_ref[g], preferred_element_type=F32)
        o_ref[:, sl] = (y * ps_ref[:, sl]).astype(o_ref.dtype)


def _pool(zh, pool_w, pool_scale, bsz, seq, col0, rows):
    t = zh.shape[0]
    ngroups, group, _ = pool_w.shape
    width = ngroups * group
    nb = seq // rows
    hb = rows // HALO
    return pl.pallas_call(
        functools.partial(_pool_kernel, rows=rows, group=group), grid=(bsz, nb),
        in_specs=[pl.BlockSpec((rows, width), lambda b, i: (b * nb + i, col0)),
                  pl.BlockSpec((HALO, width), lambda b, i: (jnp.maximum((b * nb + i) * hb - 1, 0), col0)),
                  pl.BlockSpec((ngroups, group, group), lambda b, i: (0, 0, 0)),
                  pl.BlockSpec((1, width), lambda b, i: (0, 0))],
        out_specs=pl.BlockSpec((rows, width), lambda b, i: (b * nb + i, 0)),
        out_shape=jax.ShapeDtypeStruct((t, width), BF16),
        scratch_shapes=[pltpu.VMEM((2 * HALO + rows, group), F32)],
        compiler_params=_params(2), name="pool")(zh, zh, pool_w, pool_scale)


def _ckv_kernel(z_ref, g_ref, o_ref, *, d_lat):
    c = z_ref[:, 0:d_lat]
    ms = jnp.mean(c * c, axis=-1, keepdims=True)
    o_ref[...] = (c * lax.rsqrt(ms + RMS_EPS) * g_ref[...]).astype(o_ref.dtype)


def _ckv(zs, kv_norm, d_lat, rows):
    t, ns = zs.shape
    return pl.pallas_call(
        functools.partial(_ckv_kernel, d_lat=d_lat), grid=(t // rows,),
        in_specs=[pl.BlockSpec((rows, ns), lambda i: (i, 0)),
                  pl.BlockSpec((1, d_lat), lambda i: (0, 0))],
        out_specs=pl.BlockSpec((rows, d_lat), lambda i: (i, 0)),
        out_shape=jax.ShapeDtypeStruct((t, d_lat), BF16),
        compiler_params=_params(1), name="ckv_norm")(zs, kv_norm)


def _count_ge(key_ref, nkc, ts, tq, pred):
    def body(kc, part):
        m = pred(key_ref[:, pl.ds(pl.multiple_of(kc * ts, ts), ts)], kc).astype(F32)
        for c in range(ts // LANES):
            part = part + m[:, c * LANES:(c + 1) * LANES]
        return part

    part = lax.fori_loop(0, nkc, body, jnp.zeros((tq, LANES), F32))
    return jnp.sum(part, axis=1, keepdims=True)


def _kth_largest_key(keyt_ref, nkc, ts, tq, k, few_keys):
    sign = jnp.int32(-2 ** 31)

    def count_ge(cand):
        def body(kc, part):
            m = (keyt_ref[pl.ds(pl.multiple_of(kc * ts, ts), ts), :] >= cand).astype(F32)
            for r in range(ts // COUNT_ROWS):
                part = part + m[r * COUNT_ROWS:(r + 1) * COUNT_ROWS, :]
            return part

        part = lax.fori_loop(0, nkc, body, jnp.zeros((COUNT_ROWS, tq), F32))
        return jnp.sum(part, axis=0, keepdims=True)

    def body(b, carry):
        v, n_v = carry
        cand = v | lax.shift_left(jnp.int32(1), jnp.int32(31) - b)
        n_cand = count_ge(cand ^ sign)
        keep = n_cand >= k
        return jnp.where(keep, cand, v), jnp.where(keep, n_cand, n_v)

    n_all = jnp.broadcast_to((nkc * ts).astype(F32), (1, tq))
    carry = lax.fori_loop(0, EARLY_EXIT_BIT, body, (jnp.zeros((1, tq), jnp.int32), n_all))

    def unresolved(carry):
        _, n_v = carry
        return jnp.max(jnp.where((n_v == k) | few_keys, 0.0, 1.0)) > 0.0

    def cond(state):
        b, go, _ = state
        return jnp.logical_and(b < 32, go)

    def step(state):
        b, _, carry = state
        carry = body(b + 1, body(b, carry))
        return b + 2, unresolved(carry), carry

    _, _, (v, n_v) = lax.while_loop(cond, step, (jnp.int32(EARLY_EXIT_BIT), unresolved(carry), carry))
    return v ^ sign, n_v


def _bisect_largest(nbits, target, count_fn, tq):
    sign = jnp.int32(-2 ** 31) if nbits == 32 else jnp.int32(0)

    def body(b, v):
        bit = lax.shift_left(jnp.int32(1), jnp.int32(nbits - 1) - b)
        cand = v | bit
        cnt = count_fn(cand ^ sign)
        return jnp.where(cnt >= target, cand, v)

    v = lax.fori_loop(0, nbits, body, jnp.zeros((tq, 1), jnp.int32))
    return v ^ sign


def _attn_kernel(qi_ref, zq_ref, zk_ref, ckv_ref, q_ref, wuk_ref, wuv_ref, o_ref,
                 key_ref, keyt_ref, cut_ref, bias_ref, qs_ref, sc_ref, cmax_ref, m_ref, l_ref, a_ref, acc_ref, *, tq, ts,
                 seq, n_idx,
                 n_heads, d_lat, head_dim, k_sel, idx_scale, sm_scale, slab, grp):
    i = pl.program_id(1)
    q0 = i * tq
    nkc = (q0 + tq + ts - 1) // ts
    nbits_idx = max(1, (seq - 1).bit_length())
    row = q0 + lax.broadcasted_iota(jnp.int32, (tq, 1), 0)
    int_min = jnp.int32(-2 ** 31)

    wq = zq_ref[:, IDX_DIM:IDX_DIM + n_idx] * idx_scale

    def score_body(kc, carry):
        s0 = pl.multiple_of(kc * ts, ts)
        ki = zk_ref[pl.ds(s0, ts), 0:IDX_DIM].astype(BF16)
        col = s0 + lax.broadcasted_iota(jnp.int32, (SCORE_ROWS, ts), 1)
        for r0 in range(0, tq, SCORE_ROWS):
            acc = jnp.zeros((SCORE_ROWS, ts), F32)
            for h in range(n_idx):
                lg = lax.dot_general(qi_ref[r0:r0 + SCORE_ROWS, h * IDX_DIM:(h + 1) * IDX_DIM], ki,
                                     (((1,), (1,)), ((), ())), preferred_element_type=F32)
                acc = acc + wq[r0:r0 + SCORE_ROWS, h:h + 1] * jnp.maximum(lg, 0.0)
            bits = lax.bitcast_convert_type(acc, jnp.int32)
            key = bits ^ (lax.shift_right_arithmetic(bits, 31) & jnp.int32(0x7FFFFFFF))
            key = jnp.where(col <= row[r0:r0 + SCORE_ROWS], key, int_min)
            key_ref[r0:r0 + SCORE_ROWS, pl.ds(s0, ts)] = key
            keyt_ref[pl.ds(s0, ts), r0:r0 + SCORE_ROWS] = key.T
        return carry

    lax.fori_loop(0, nkc, score_body, 0)

    kf = jnp.float32(k_sel)
    few_keys = (q0 + lax.broadcasted_iota(jnp.int32, (1, tq), 1)) + 1 < k_sel
    thr_t, n_ge = _kth_largest_key(keyt_ref, nkc, ts, tq, kf, few_keys)
    thr = jnp.broadcast_to(thr_t, (SUBLANES, tq)).T[:, 0:1]

    def rev_idx(kc):
        return (seq - 1) - (kc * ts + lax.broadcasted_iota(jnp.int32, (tq, ts), 1))

    cut_ref[...] = jnp.zeros((tq, LANES), jnp.int32)

    @pl.when(jnp.max(jnp.where(thr_t > int_min, n_ge, 0.0)) > kf)
    def _():
        n_gt = _count_ge(key_ref, nkc, ts, tq, lambda x, kc: x > thr)
        need = kf - n_gt
        c = _bisect_largest(
            nbits_idx, need,
            lambda v: _count_ge(key_ref, nkc, ts, tq, lambda x, kc: (x == thr) & (rev_idx(kc) >= v)), tq)
        cut_ref[...] = jnp.broadcast_to(c, (tq, LANES))

    cut = cut_ref[:, 0:1]

    rows_all = n_heads * tq
    nlc = ts // LANES
    k2 = sm_scale * LOG2E
    m_ref[...] = jnp.full((rows_all, LANES), NEG_BIG, F32)
    l_ref[...] = jnp.zeros((rows_all, LANES), F32)
    acc_ref[...] = jnp.zeros((rows_all, d_lat), F32)
    for h in range(n_heads):
        qs_ref[h * tq:(h + 1) * tq, :] = jnp.dot(q_ref[:, h * head_dim:(h + 1) * head_dim], wuk_ref[h],
                                                 preferred_element_type=F32).astype(BF16)
    n_groups = rows_all // grp
    unroll = min(GROUP_UNROLL, n_groups)

    def start_chunk(kc, slot):
        s0 = pl.multiple_of(kc * ts, ts)
        x = key_ref[:, pl.ds(s0, ts)]
        col = s0 + lax.broadcasted_iota(jnp.int32, (tq, ts), 1)
        sel = ((x > thr) | ((x == thr) & (rev_idx(kc) >= cut))) & (col <= row)
        bias_ref[...] = jnp.where(sel, 0.0, NEG_BIG)
        ckv = ckv_ref[pl.ds(s0, ts), :]
        for g in range(n_groups):
            g0 = g * grp
            sc = lax.dot_general(qs_ref[g0:g0 + grp, :], ckv, (((1,), (1,)), ((), ())),
                                 preferred_element_type=F32)
            for s in range(grp // slab):
                r0, b0 = s * slab, (s * slab) % tq
                xs = [sc[r0:r0 + slab, c * LANES:(c + 1) * LANES] * k2
                      + bias_ref[b0:b0 + slab, c * LANES:(c + 1) * LANES] for c in range(nlc)]
                for c in range(nlc):
                    sc_ref[slot, g0 + r0:g0 + r0 + slab, c * LANES:(c + 1) * LANES] = xs[c]
                cmax_ref[slot, g0 + r0:g0 + r0 + slab, :] = functools.reduce(jnp.maximum, xs)

    def rescale(slot):
        m_old = m_ref[...]
        m_new = jnp.maximum(m_old, jnp.max(cmax_ref[slot], axis=1, keepdims=True))
        a_ref[...] = jnp.exp2(m_old - m_new)
        m_ref[...] = m_new

    def finish_chunk(kc, slot):
        ckv = ckv_ref[pl.ds(pl.multiple_of(kc * ts, ts), ts), :]
        for g in range(n_groups):
            g0 = g * grp
            probs = []
            for s in range(grp // slab):
                rows_s = slice(g0 + s * slab, g0 + (s + 1) * slab)
                m = m_ref[rows_s, :]
                ps = [jnp.exp2(sc_ref[slot, rows_s, c * LANES:(c + 1) * LANES] - m) for c in range(nlc)]
                l_ref[rows_s, :] = a_ref[rows_s, :] * l_ref[rows_s, :] + functools.reduce(lambda a, b: a + b, ps)
                probs.append(jnp.concatenate([pc.astype(BF16) for pc in ps], axis=1))
            pv = jnp.dot(jnp.concatenate(probs, axis=0), ckv, preferred_element_type=F32)
            a = a_ref[g0:g0 + grp, :]
            acc_ref[g0:g0 + grp, :] = (acc_ref[g0:g0 + grp, :] * jnp.concatenate([a] * (d_lat // LANES), axis=1) + pv)

    start_chunk(0, 0)

    def pipe_body(kc, carry):
        for slot in (0, 1):
            @pl.when(kc % 2 == slot)
            def _(slot=slot):
                rescale(slot)
                finish_chunk(kc, slot)
                start_chunk(kc + 1, 1 - slot)
        return carry

    lax.fori_loop(0, nkc - 1, pipe_body, 0)
    for slot in (0, 1):
        @pl.when((nkc - 1) % 2 == slot)
        def _(slot=slot):
            rescale(slot)
            finish_chunk(nkc - 1, slot)

    for h in range(n_heads):
        l = jnp.sum(l_ref[h * tq:(h + 1) * tq, :], axis=1, keepdims=True)
        o_lat = (acc_ref[h * tq:(h + 1) * tq, :] / l).astype(BF16)
        o_ref[:, h * head_dim:(h + 1) * head_dim] = jnp.dot(
            o_lat, wuv_ref[h], preferred_element_type=F32).astype(o_ref.dtype)


def _attention(zh, zs, ckv, w_uk, w_uv, bsz, seq, qi_col, q_col, n_idx, kw_col, tq, ts, slab, grp):
    t = zh.shape[0]
    nh, dl, hd = w_uv.shape
    nq = seq // tq
    k_sel = min(TOPK_MAX, seq // 4)
    assert ts >= k_sel and seq % ts == 0 and seq % tq == 0 and ts % tq == 0
    assert grp % tq == 0 and (nh * tq) % grp == 0 and tq % slab == 0 and slab % BF16_ROWS == 0
    assert tq % SCORE_ROWS == 0
    kern = functools.partial(
        _attn_kernel, tq=tq, ts=ts, seq=seq, n_idx=n_idx, n_heads=nh, d_lat=dl, head_dim=hd, k_sel=k_sel,
        idx_scale=float(IDX_DIM ** -0.5 * n_idx ** -0.5), sm_scale=float(hd ** -0.5), slab=slab, grp=grp)
    return pl.pallas_call(
        kern, grid=(bsz, nq),
        in_specs=[pl.BlockSpec((tq, n_idx * IDX_DIM), lambda b, i: (b * nq + i, qi_col)),
                  pl.BlockSpec((tq, LANES), lambda b, i: (b * nq + i, kw_col)),
                  pl.BlockSpec((seq, LANES), lambda b, i: (b, kw_col)),
                  pl.BlockSpec((seq, dl), lambda b, i: (b, 0)),
                  pl.BlockSpec((tq, nh * hd), lambda b, i: (b * nq + i, q_col)),
                  pl.BlockSpec((nh, hd, dl), lambda b, i: (0, 0, 0)),
                  pl.BlockSpec((nh, dl, hd), lambda b, i: (0, 0, 0))],
        out_specs=pl.BlockSpec((tq, nh * hd), lambda b, i: (b * nq + i, 0)),
        out_shape=jax.ShapeDtypeStruct((t, nh * hd), BF16),
        scratch_shapes=[pltpu.VMEM((tq, seq), jnp.int32),
                        pltpu.VMEM((seq, tq), jnp.int32),
                        pltpu.VMEM((tq, LANES), jnp.int32),
                        pltpu.VMEM((tq, ts), F32),
                        pltpu.VMEM((nh * tq, dl), BF16),
                        pltpu.VMEM((2, nh * tq, ts), F32),
                        pltpu.VMEM((2, nh * tq, LANES), F32),
                        pltpu.VMEM((nh * tq, LANES), F32),
                        pltpu.VMEM((nh * tq, LANES), F32),
                        pltpu.VMEM((nh * tq, LANES), F32),
                        pltpu.VMEM((nh * tq, dl), F32)],
        compiler_params=_params(2), name="dsa_attention")(zh, zs, zs, ckv, zh, w_uk, w_uv)


def _merge_kernel(ya_ref, yb_ref, yc_ref, wa_ref, wb_ref, wc_ref, g0_ref, g1_ref, g2_ref, b_ref, o_ref):
    b = b_ref[...]
    tm = o_ref.shape[0]
    sub = min(tm, MERGE_SUB_ROWS)
    for r0 in range(0, tm, sub):
        out = None
        for k, (y_ref, w_ref, g_ref) in enumerate(((ya_ref, wa_ref, g0_ref), (yb_ref, wb_ref, g1_ref),
                                                    (yc_ref, wc_ref, g2_ref))):
            gate = jax.nn.sigmoid(g_ref[r0:r0 + sub, :].astype(F32) + b[k:k + 1, :])
            term = gate * jnp.dot(y_ref[r0:r0 + sub, :], w_ref[...], preferred_element_type=F32)
            out = term if out is None else out + term
        o_ref[r0:r0 + sub, :] = out.astype(o_ref.dtype)


def _merge(ya, yb, yc, wa, wb, wc, layer, zg, b_gate, tm, tn):
    t = ya.shape[0]
    d = wa.shape[2]
    nj = d // tn

    def yspec(y):
        return pl.BlockSpec((tm, y.shape[1]), lambda i, j: (i, 0))

    def wspec(w):
        return pl.BlockSpec((None, w.shape[1], tn), lambda i, j: (layer, 0, j))

    def gspec(k):
        return pl.BlockSpec((tm, tn), lambda i, j: (i, k * nj + j))

    return pl.pallas_call(
        _merge_kernel, grid=(t // tm, nj),
        in_specs=[yspec(ya), yspec(yb), yspec(yc), wspec(wa), wspec(wb), wspec(wc),
                  gspec(0), gspec(1), gspec(2), pl.BlockSpec((b_gate.shape[0], tn), lambda i, j: (0, j))],
        out_specs=pl.BlockSpec((tm, tn), lambda i, j: (i, j)),
        out_shape=jax.ShapeDtypeStruct((t, d), BF16),
        compiler_params=_params(2), name="merge")(ya, yb, yc, wa, wb, wc, zg, zg, zg, b_gate)


def _ln_kernel(y_ref, g_ref, b_ref, o_ref, ob_ref):
    y = y_ref[...]
    mu = jnp.mean(y, axis=-1, keepdims=True)
    yc = y - mu
    var = jnp.mean(yc * yc, axis=-1, keepdims=True)
    out = yc * lax.rsqrt(var + LN_EPS) * g_ref[...] + b_ref[...]
    o_ref[...] = out
    ob_ref[...] = out.astype(BF16)


def _layer_norm(y, g, b, rows):
    t, d = y.shape
    row_spec = pl.BlockSpec((rows, d), lambda i: (i, 0))
    vec_spec = pl.BlockSpec((1, d), lambda i: (0, 0))
    return pl.pallas_call(
        _ln_kernel, grid=(t // rows,), in_specs=[row_spec, vec_spec, vec_spec],
        out_specs=[row_spec, row_spec],
        out_shape=[jax.ShapeDtypeStruct((t, d), F32), jax.ShapeDtypeStruct((t, d), BF16)],
        compiler_params=_params(1), name="layer_norm")(y, g, b)


def _ffn_up_kernel(x_ref, wg_ref, wu_ref, cw_ref, o_ref, buf_ref, carry_ref, *, tm, sub, tiles_per_seq):
    i = pl.program_id(0)
    j = pl.program_id(1)
    seq_start = i % tiles_per_seq == 0

    @pl.when(seq_start)
    def _():
        buf_ref[0:SUBLANES, :] = jnp.zeros((SUBLANES, buf_ref.shape[1]), F32)

    @pl.when(jnp.logical_not(seq_start))
    def _():
        buf_ref[0:SUBLANES, :] = carry_ref[j]

    w = cw_ref[...]
    for r0 in range(0, tm, sub):
        xs = x_ref[r0:r0 + sub, :]
        gt = jnp.dot(xs, wg_ref[...], preferred_element_type=F32)
        up = jnp.dot(xs, wu_ref[...], preferred_element_type=F32)
        buf_ref[SUBLANES + r0:SUBLANES + r0 + sub, :] = gt
        if r0 + sub == tm:
            carry_ref[j] = gt[sub - SUBLANES:sub, :]
        s1 = buf_ref[SUBLANES - 1 + r0:SUBLANES - 1 + r0 + sub, :]
        s2 = buf_ref[SUBLANES - 2 + r0:SUBLANES - 2 + r0 + sub, :]
        conv = w[0:1, :] * s2 + w[1:2, :] * s1 + w[2:3, :] * gt
        o_ref[r0:r0 + sub, :] = (jax.nn.silu(conv) * up).astype(o_ref.dtype)


def _ffn_up(xb, w_up, layer, conv_w, seq, tm, tn):
    t, d = xb.shape
    dff = conv_w.shape[1]
    assert seq % tm == 0 and dff % tn == 0 and w_up.shape[2] == 2 * dff
    nj = dff // tn
    return pl.pallas_call(
        functools.partial(_ffn_up_kernel, tm=tm, sub=min(tm, FFN_SUB_ROWS), tiles_per_seq=seq // tm),
        grid=(t // tm, nj),
        in_specs=[pl.BlockSpec((tm, d), lambda i, j: (i, 0)),
                  pl.BlockSpec((None, d, tn), lambda i, j: (layer, 0, j)),
                  pl.BlockSpec((None, d, tn), lambda i, j: (layer, 0, nj + j)),
                  pl.BlockSpec((CONV_W, tn), lambda i, j: (0, j))],
        out_specs=pl.BlockSpec((tm, tn), lambda i, j: (i, j)),
        out_shape=jax.ShapeDtypeStruct((t, dff), BF16),
        scratch_shapes=[pltpu.VMEM((SUBLANES + tm, tn), F32), pltpu.VMEM((nj, SUBLANES, tn), F32)],
        compiler_params=_params(2), name="ffn_up")(xb, w_up, w_up, conv_w)


def _tiles(seq, d_model):
    big = d_model >= 4096
    return dict(
        mm=(1024, 1024) if big else (256, 512),
        down=(512, 512) if big else (256, 512),
        ffn_up=(2048, 256) if big else (256, 256),
        merge=(1024, 512) if big else (256, 512),
        rows=512 if big else 256,
        ln_rows=512 if big else 256,
        tq=128, ts=512, slab=(32, 512))


def kernel(x, w_in, b_gate, conv_a, kv_norm, w_uk, w_uv, pool_w, pool_scale, w_br_a, w_br_b, w_br_c,
           w_o, ln1_g, ln1_b, w_up, conv_ffn_w, w_down, ln2_g, ln2_b):
    bsz, seq, d = x.shape
    depth = w_in.shape[0]
    alpha = (2.0 * depth) ** 0.25
    d_conv = conv_a.shape[2]
    n_heads, head_dim, d_lat = w_uk.shape[1:]
    d_att = n_heads * head_dim
    d_pool = pool_scale.shape[1]
    d_ff = conv_ffn_w.shape[2]
    n_in = w_in.shape[2]
    n_idx = (n_in - (3 * d_conv + d_att + d_lat + IDX_DIM + d_pool + b_gate.shape[1] * d)) // (IDX_DIM + 1)
    t = bsz * seq
    tl = _tiles(seq, d)

    o_bg, o_cg, o_v = 0, d_conv, 2 * d_conv
    o_q = 3 * d_conv
    o_ckv = o_q + d_att
    o_qi = o_ckv + d_lat
    o_ki = o_qi + n_idx * IDX_DIM
    o_wi = o_ki + IDX_DIM
    o_up = o_wi + n_idx
    o_g = o_up + d_pool
    assert o_g + b_gate.shape[1] * d == n_in
    assert IDX_DIM + n_idx <= LANES

    h_qi, h_q, h_a, h_up = 0, n_idx * IDX_DIM, n_idx * IDX_DIM + d_att, n_idx * IDX_DIM + d_att + 3 * d_conv
    n_h = h_up + d_pool
    assert h_q % d_att == 0 and h_a % d_conv == 0 and h_up % d_pool == 0 and h_qi % (n_idx * IDX_DIM) == 0
    n_s = d_lat + LANES

    wi = w_in.astype(BF16)
    w_h = jnp.concatenate([wi[:, :, o_qi:o_ki], wi[:, :, o_q:o_ckv], wi[:, :, o_bg:o_q], wi[:, :, o_up:o_g]], axis=2)
    w_s = jnp.concatenate([wi[:, :, o_ckv:o_qi], wi[:, :, o_ki:o_up],
                           jnp.zeros((depth, d, LANES - IDX_DIM - n_idx), BF16)], axis=2)
    w_g = wi[:, :, o_g:]
    wa_b, wb_b, wc_b = w_br_a.astype(BF16), w_br_b.astype(BF16), w_br_c.astype(BF16)
    wo_b, wup_b, wdn_b = w_o.astype(BF16), w_up.astype(BF16), w_down.astype(BF16)

    xf = x.reshape(t, d)
    xb = xf.astype(BF16)
    for l in range(depth):
        zh = _matmul(xb, w_h, l, BF16, *tl["mm"], name="proj_wide")
        zs = _matmul(xb, w_s, l, F32, tl["mm"][0], n_s, name="proj_small")
        zg = _matmul(xb, w_g, l, BF16, *tl["mm"], name="proj_gates")

        ya = _conv_a(zh, conv_a[l], bsz, seq, h_a // d_conv, d_conv, tl["rows"])
        yc = _pool(zh, pool_w[l].astype(BF16), pool_scale[l].reshape(1, d_pool), bsz, seq, h_up // d_pool,
                   tl["rows"])
        ckv = _ckv(zs, kv_norm[l].reshape(1, d_lat), d_lat, tl["rows"])
        yb = _attention(zh, zs, ckv, w_uk[l].astype(BF16), w_uv[l].astype(BF16), bsz, seq,
                        h_qi // (n_idx * IDX_DIM), h_q // d_att, n_idx, d_lat // LANES,
                        tl["tq"], min(tl["ts"], seq), *tl["slab"])

        merged = _merge(ya, yb, yc, wa_b, wb_b, wc_b, l, zg, b_gate[l], *tl["merge"])
        y1 = _matmul(merged, wo_b, l, F32, *tl["mm"], res=xf, alpha=alpha, name="proj_out")
        xf, xb = _layer_norm(y1, ln1_g[l].reshape(1, d), ln1_b[l].reshape(1, d), tl["ln_rows"])

        h = _ffn_up(xb, wup_b, l, conv_ffn_w[l], seq, *tl["ffn_up"])
        y2 = _matmul(h, wdn_b, l, F32, *tl["down"], res=xf, alpha=alpha, name="ffn_down")
        xf, xb = _layer_norm(y2, ln2_g[l].reshape(1, d), ln2_b[l].reshape(1, d), tl["ln_rows"])
    return xf.reshape(bsz, seq, d)
```

```python
import functools

import jax
import jax.numpy as jnp
from jax import lax
from jax.experimental import pallas as pl
from jax.experimental.pallas import tpu as pltpu

IDX_DIM = 64
TOPK_MAX = 256
POOL_WINDOWS = (2, 4, 8, 16)
CONV_W = 3
LN_EPS = 1e-5
RMS_EPS = 1e-6

LANES = 128
SUBLANES = 8
BF16_ROWS = 16
VMEM_BYTES_V7X = 64 * 1024 * 1024
VMEM_LIMIT = VMEM_BYTES_V7X - 8 * 1024 * 1024

HALO = BF16_ROWS
SCORE_ROWS = 128
EARLY_EXIT_BIT = 22
COUNT_ROWS = 32
MERGE_SUB_ROWS = 256
FFN_SUB_ROWS = 256
NEG_BIG = -1e30
LOG2E = 1.4426950408889634
F32 = jnp.float32
BF16 = jnp.bfloat16


def _params(n_axes):
    return pltpu.CompilerParams(dimension_semantics=("arbitrary",) * n_axes,
                                vmem_limit_bytes=VMEM_LIMIT)


def _mm_kernel(x_ref, w_ref, o_ref):
    o_ref[...] = jnp.dot(x_ref[...], w_ref[...], preferred_element_type=F32).astype(o_ref.dtype)


def _mm_res_kernel(x_ref, w_ref, r_ref, o_ref, *, alpha):
    acc = jnp.dot(x_ref[...], w_ref[...], preferred_element_type=F32)
    o_ref[...] = (alpha * r_ref[...] + acc).astype(o_ref.dtype)


def _matmul(x, w, layer, out_dtype, tm, tn, res=None, alpha=None, name="mm"):
    m, k = x.shape
    n = w.shape[2]
    tm, tn = min(tm, m), min(tn, n)
    assert m % tm == 0 and n % tn == 0, (m, n, tm, tn)
    in_specs = [pl.BlockSpec((tm, k), lambda i, j: (i, 0)),
                pl.BlockSpec((None, k, tn), lambda i, j: (layer, 0, j))]
    args = [x, w]
    kern = _mm_kernel
    if res is not None:
        in_specs.append(pl.BlockSpec((tm, tn), lambda i, j: (i, j)))
        args.append(res)
        kern = functools.partial(_mm_res_kernel, alpha=alpha)
    return pl.pallas_call(
        kern, grid=(m // tm, n // tn), in_specs=in_specs,
        out_specs=pl.BlockSpec((tm, tn), lambda i, j: (i, j)),
        out_shape=jax.ShapeDtypeStruct((m, n), out_dtype),
        compiler_params=_params(2), name=name)(*args)


def _conv_a_kernel(bg_ref, cg_ref, v_ref, cgp_ref, vp_ref, w_ref, o_ref, buf_ref, *, rows):
    i = pl.program_id(1)
    u = cg_ref[...].astype(F32) * v_ref[...].astype(F32)
    up = cgp_ref[...].astype(F32) * vp_ref[...].astype(F32)
    buf_ref[0:HALO, :] = jnp.where(i > 0, up, 0.0)
    buf_ref[HALO:HALO + rows, :] = u
    s1 = buf_ref[HALO - 1:HALO - 1 + rows, :]
    s2 = buf_ref[HALO - 2:HALO - 2 + rows, :]
    w = w_ref[...]
    conv = w[0:1, :] * s2 + w[1:2, :] * s1 + w[2:3, :] * u
    o_ref[...] = (bg_ref[...].astype(F32) * conv).astype(o_ref.dtype)


def _conv_a(zh, conv_w, bsz, seq, col0, width, rows):
    t = zh.shape[0]
    nb = seq // rows
    hb = rows // HALO

    def cur(c):
        return pl.BlockSpec((rows, width), lambda b, i: (b * nb + i, col0 + c))

    def prev(c):
        return pl.BlockSpec((HALO, width), lambda b, i: (jnp.maximum((b * nb + i) * hb - 1, 0), col0 + c))

    return pl.pallas_call(
        functools.partial(_conv_a_kernel, rows=rows), grid=(bsz, nb),
        in_specs=[cur(0), cur(1), cur(2), prev(1), prev(2),
                  pl.BlockSpec((CONV_W, width), lambda b, i: (0, 0))],
        out_specs=pl.BlockSpec((rows, width), lambda b, i: (b * nb + i, 0)),
        out_shape=jax.ShapeDtypeStruct((t, width), BF16),
        scratch_shapes=[pltpu.VMEM((HALO + rows, width), F32)],
        compiler_params=_params(2), name="conv_a")(zh, zh, zh, zh, zh, conv_w)


def _pool_kernel(u_ref, up_ref, pw_ref, ps_ref, o_ref, buf_ref, *, rows, group):
    i = pl.program_id(1)
    ext = HALO + rows
    pos = i * rows + lax.broadcasted_iota(jnp.int32, (rows, 1), 0)
    buf_ref[0:HALO, :] = jnp.zeros((HALO, group), F32)
    for g, win in enumerate(POOL_WINDOWS):
        sl = slice(g * group, (g + 1) * group)
        u = u_ref[:, sl].astype(F32)
        buf_ref[HALO:2 * HALO, :] = jnp.where(i > 0, up_ref[:, sl].astype(F32), 0.0)
        buf_ref[2 * HALO:2 * HALO + rows, :] = u
        k = 1
        while k < win:
            s = buf_ref[HALO:HALO + ext, :] + buf_ref[HALO - k:HALO - k + ext, :]
            buf_ref[HALO:HALO + ext, :] = s
            k *= 2
        wsum = buf_ref[2 * HALO:2 * HALO + rows, :]
        cnt = jnp.minimum(pos + 1, win).astype(F32)
        d = wsum / cnt - u
        y = jnp.dot(d.astype(BF16), pw_ref[g], preferred_element_type=F32)
        o_ref[:, sl] = (y * ps_ref[:, sl]).astype(o_ref.dtype)


def _pool(zh, pool_w, pool_scale, bsz, seq, col0, rows):
    t = zh.shape[0]
    ngroups, group, _ = pool_w.shape
    width = ngroups * group
    nb = seq // rows
    hb = rows // HALO
    return pl.pallas_call(
        functools.partial(_pool_kernel, rows=rows, group=group), grid=(bsz, nb),
        in_specs=[pl.BlockSpec((rows, width), lambda b, i: (b * nb + i, col0)),
                  pl.BlockSpec((HALO, width), lambda b, i: (jnp.maximum((b * nb + i) * hb - 1, 0), col0)),
                  pl.BlockSpec((ngroups, group, group), lambda b, i: (0, 0, 0)),
                  pl.BlockSpec((1, width), lambda b, i: (0, 0))],
        out_specs=pl.BlockSpec((rows, width), lambda b, i: (b * nb + i, 0)),
        out_shape=jax.ShapeDtypeStruct((t, width), BF16),
        scratch_shapes=[pltpu.VMEM((2 * HALO + rows, group), F32)],
        compiler_params=_params(2), name="pool")(zh, zh, pool_w, pool_scale)


def _ckv_kernel(z_ref, g_ref, o_ref, *, d_lat):
    c = z_ref[:, 0:d_lat]
    ms = jnp.mean(c * c, axis=-1, keepdims=True)
    o_ref[...] = (c * lax.rsqrt(ms + RMS_EPS) * g_ref[...]).astype(o_ref.dtype)


def _ckv(zs, kv_norm, d_lat, rows):
    t, ns = zs.shape
    return pl.pallas_call(
        functools.partial(_ckv_kernel, d_lat=d_lat), grid=(t // rows,),
        in_specs=[pl.BlockSpec((rows, ns), lambda i: (i, 0)),
                  pl.BlockSpec((1, d_lat), lambda i: (0, 0))],
        out_specs=pl.BlockSpec((rows, d_lat), lambda i: (i, 0)),
        out_shape=jax.ShapeDtypeStruct((t, d_lat), BF16),
        compiler_params=_params(1), name="ckv_norm")(zs, kv_norm)


def _count_ge(key_ref, nkc, ts, tq, pred):
    def body(kc, part):
        m = pred(key_ref[:, pl.ds(pl.multiple_of(kc * ts, ts), ts)], kc).astype(F32)
        for c in range(ts // LANES):
            part = part + m[:, c * LANES:(c + 1) * LANES]
        return part

    part = lax.fori_loop(0, nkc, body, jnp.zeros((tq, LANES), F32))
    return jnp.sum(part, axis=1, keepdims=True)


def _kth_largest_key(keyt_ref, nkc, ts, tq, k, few_keys):
    sign = jnp.int32(-2 ** 31)

    def count_ge(cand):
        def body(kc, part):
            m = (keyt_ref[pl.ds(pl.multiple_of(kc * ts, ts), ts), :] >= cand).astype(F32)
            for r in range(ts // COUNT_ROWS):
                part = part + m[r * COUNT_ROWS:(r + 1) * COUNT_ROWS, :]
            return part

        part = lax.fori_loop(0, nkc, body, jnp.zeros((COUNT_ROWS, tq), F32))
        return jnp.sum(part, axis=0, keepdims=True)

    def body(b, carry):
        v, n_v = carry
        cand = v | lax.shift_left(jnp.int32(1), jnp.int32(31) - b)
        n_cand = count_ge(cand ^ sign)
        keep = n_cand >= k
        return jnp.where(keep, cand, v), jnp.where(keep, n_cand, n_v)

    n_all = jnp.broadcast_to((nkc * ts).astype(F32), (1, tq))
    carry = lax.fori_loop(0, EARLY_EXIT_BIT, body, (jnp.zeros((1, tq), jnp.int32), n_all))

    def unresolved(carry):
        _, n_v = carry
        return jnp.max(jnp.where((n_v == k) | few_keys, 0.0, 1.0)) > 0.0

    def cond(state):
        b, go, _ = state
        return jnp.logical_and(b < 32, go)

    def step(state):
        b, _, carry = state
        carry = body(b + 1, body(b, carry))
        return b + 2, unresolved(carry), carry

    _, _, (v, n_v) = lax.while_loop(cond, step, (jnp.int32(EARLY_EXIT_BIT), unresolved(carry), carry))
    return v ^ sign, n_v


def _bisect_largest(nbits, target, count_fn, tq):
    sign = jnp.int32(-2 ** 31) if nbits == 32 else jnp.int32(0)

    def body(b, v):
        bit = lax.shift_left(jnp.int32(1), jnp.int32(nbits - 1) - b)
        cand = v | bit
        cnt = count_fn(cand ^ sign)
        return jnp.where(cnt >= target, cand, v)

    v = lax.fori_loop(0, nbits, body, jnp.zeros((tq, 1), jnp.int32))
    return v ^ sign


def _attn_kernel(qi_ref, zq_ref, zk_ref, ckv_ref, q_ref, wuk_ref, wuv_ref, o_ref,
                 key_ref, keyt_ref, cut_ref, bias_ref, qs_ref, sc_ref, cmax_ref, m_ref, l_ref, a_ref, acc_ref, *, tq, ts,
                 seq, n_idx,
                 n_heads, d_lat, head_dim, k_sel, idx_scale, sm_scale, slab, grp):
    i = pl.program_id(1)
    q0 = i * tq
    nkc = (q0 + tq + ts - 1) // ts
    nbits_idx = max(1, (seq - 1).bit_length())
    row = q0 + lax.broadcasted_iota(jnp.int32, (tq, 1), 0)
    int_min = jnp.int32(-2 ** 31)

    wq = zq_ref[:, IDX_DIM:IDX_DIM + n_idx] * idx_scale

    def score_body(kc, carry):
        s0 = pl.multiple_of(kc * ts, ts)
        ki = zk_ref[pl.ds(s0, ts), 0:IDX_DIM].astype(BF16)
        col = s0 + lax.broadcasted_iota(jnp.int32, (SCORE_ROWS, ts), 1)
        for r0 in range(0, tq, SCORE_ROWS):
            acc = jnp.zeros((SCORE_ROWS, ts), F32)
            for h in range(n_idx):
                lg = lax.dot_general(qi_ref[r0:r0 + SCORE_ROWS, h * IDX_DIM:(h + 1) * IDX_DIM], ki,
                                     (((1,), (1,)), ((), ())), preferred_element_type=F32)
                acc = acc + wq[r0:r0 + SCORE_ROWS, h:h + 1] * jnp.maximum(lg, 0.0)
            bits = lax.bitcast_convert_type(acc, jnp.int32)
            key = bits ^ (lax.shift_right_arithmetic(bits, 31) & jnp.int32(0x7FFFFFFF))
            key = jnp.where(col <= row[r0:r0 + SCORE_ROWS], key, int_min)
            key_ref[r0:r0 + SCORE_ROWS, pl.ds(s0, ts)] = key
            keyt_ref[pl.ds(s0, ts), r0:r0 + SCORE_ROWS] = key.T
        return carry

    lax.fori_loop(0, nkc, score_body, 0)

    kf = jnp.float32(k_sel)
    few_keys = (q0 + lax.broadcasted_iota(jnp.int32, (1, tq), 1)) + 1 < k_sel
    thr_t, n_ge = _kth_largest_key(keyt_ref, nkc, ts, tq, kf, few_keys)
    thr = jnp.broadcast_to(thr_t, (SUBLANES, tq)).T[:, 0:1]

    def rev_idx(kc):
        return (seq - 1) - (kc * ts + lax.broadcasted_iota(jnp.int32, (tq, ts), 1))

    cut_ref[...] = jnp.zeros((tq, LANES), jnp.int32)

    @pl.when(jnp.max(jnp.where(thr_t > int_min, n_ge, 0.0)) > kf)
    def _():
        n_gt = _count_ge(key_ref, nkc, ts, tq, lambda x, kc: x > thr)
        need = kf - n_gt
        c = _bisect_largest(
            nbits_idx, need,
            lambda v: _count_ge(key_ref, nkc, ts, tq, lambda x, kc: (x == thr) & (rev_idx(kc) >= v)), tq)
        cut_ref[...] = jnp.broadcast_to(c, (tq, LANES))

    cut = cut_ref[:, 0:1]

    rows_all = n_heads * tq
    nlc = ts // LANES
    k2 = sm_scale * LOG2E
    m_ref[...] = jnp.full((rows_all, LANES), NEG_BIG, F32)
    l_ref[...] = jnp.zeros((rows_all, LANES), F32)
    acc_ref[...] = jnp.zeros((rows_all, d_lat), F32)
    for h in range(n_heads):
        qs_ref[h * tq:(h + 1) * tq, :] = jnp.dot(q_ref[:, h * head_dim:(h + 1) * head_dim], wuk_ref[h],
                                                 preferred_element_type=F32).astype(BF16)
    n_groups = rows_all // grp

    def start_chunk(kc, slot):
        s0 = pl.multiple_of(kc * ts, ts)
        x = key_ref[:, pl.ds(s0, ts)]
        col = s0 + lax.broadcasted_iota(jnp.int32, (tq, ts), 1)
        sel = ((x > thr) | ((x == thr) & (rev_idx(kc) >= cut))) & (col <= row)
        bias_ref[...] = jnp.where(sel, 0.0, NEG_BIG)
        ckv = ckv_ref[pl.ds(s0, ts), :]
        for g in range(n_groups):
            g0 = g * grp
            sc = lax.dot_general(qs_ref[g0:g0 + grp, :], ckv, (((1,), (1,)), ((), ())),
                                 preferred_element_type=F32)
            for s in range(grp // slab):
                r0, b0 = s * slab, (s * slab) % tq
                xs = [sc[r0:r0 + slab, c * LANES:(c + 1) * LANES] * k2
                      + bias_ref[b0:b0 + slab, c * LANES:(c + 1) * LANES] for c in range(nlc)]
                for c in range(nlc):
                    sc_ref[slot, g0 + r0:g0 + r0 + slab, c * LANES:(c + 1) * LANES] = xs[c]
                cmax_ref[slot, g0 + r0:g0 + r0 + slab, :] = functools.reduce(jnp.maximum, xs)

    def rescale(slot):
        m_old = m_ref[...]
        m_new = jnp.maximum(m_old, jnp.max(cmax_ref[slot], axis=1, keepdims=True))
        a_ref[...] = jnp.exp2(m_old - m_new)
        m_ref[...] = m_new

    def finish_chunk(kc, slot):
        ckv = ckv_ref[pl.ds(pl.multiple_of(kc * ts, ts), ts), :]
        for g in range(n_groups):
            g0 = g * grp
            probs = []
            for s in range(grp // slab):
                rows_s = slice(g0 + s * slab, g0 + (s + 1) * slab)
                m = m_ref[rows_s, :]
                ps = [jnp.exp2(sc_ref[slot, rows_s, c * LANES:(c + 1) * LANES] - m) for c in range(nlc)]
                l_ref[rows_s, :] = a_ref[rows_s, :] * l_ref[rows_s, :] + functools.reduce(lambda a, b: a + b, ps)
                probs.append(jnp.concatenate([pc.astype(BF16) for pc in ps], axis=1))
            pv = jnp.dot(jnp.concatenate(probs, axis=0), ckv, preferred_element_type=F32)
            a = a_ref[g0:g0 + grp, :]
            acc_ref[g0:g0 + grp, :] = (acc_ref[g0:g0 + grp, :] * jnp.concatenate([a] * (d_lat // LANES), axis=1) + pv)

    start_chunk(0, 0)

    def pipe_body(kc, carry):
        for slot in (0, 1):
            @pl.when(kc % 2 == slot)
            def _(slot=slot):
                rescale(slot)
                finish_chunk(kc, slot)
                start_chunk(kc + 1, 1 - slot)
        return carry

    lax.fori_loop(0, nkc - 1, pipe_body, 0)
    for slot in (0, 1):
        @pl.when((nkc - 1) % 2 == slot)
        def _(slot=slot):
            rescale(slot)
            finish_chunk(nkc - 1, slot)

    for h in range(n_heads):
        l = jnp.sum(l_ref[h * tq:(h + 1) * tq, :], axis=1, keepdims=True)
        o_lat = (acc_ref[h * tq:(h + 1) * tq, :] / l).astype(BF16)
        o_ref[:, h * head_dim:(h + 1) * head_dim] = jnp.dot(
            o_lat, wuv_ref[h], preferred_element_type=F32).astype(o_ref.dtype)


def _attention(zh, zs, ckv, w_uk, w_uv, bsz, seq, qi_col, q_col, n_idx, kw_col, tq, ts, slab, grp):
    t = zh.shape[0]
    nh, dl, hd = w_uv.shape
    nq = seq // tq
    k_sel = min(TOPK_MAX, seq // 4)
    assert ts >= k_sel and seq % ts == 0 and seq % tq == 0 and ts % tq == 0
    assert grp % tq == 0 and (nh * tq) % grp == 0 and tq % slab == 0 and slab % BF16_ROWS == 0
    assert tq % SCORE_ROWS == 0
    kern = functools.partial(
        _attn_kernel, tq=tq, ts=ts, seq=seq, n_idx=n_idx, n_heads=nh, d_lat=dl, head_dim=hd, k_sel=k_sel,
        idx_scale=float(IDX_DIM ** -0.5 * n_idx ** -0.5), sm_scale=float(hd ** -0.5), slab=slab, grp=grp)
    return pl.pallas_call(
        kern, grid=(bsz, nq),
        in_specs=[pl.BlockSpec((tq, n_idx * IDX_DIM), lambda b, i: (b * nq + i, qi_col)),
                  pl.BlockSpec((tq, LANES), lambda b, i: (b * nq + i, kw_col)),
                  pl.BlockSpec((seq, LANES), lambda b, i: (b, kw_col)),
                  pl.BlockSpec((seq, dl), lambda b, i: (b, 0)),
                  pl.BlockSpec((tq, nh * hd), lambda b, i: (b * nq + i, q_col)),
                  pl.BlockSpec((nh, hd, dl), lambda b, i: (0, 0, 0)),
                  pl.BlockSpec((nh, dl, hd), lambda b, i: (0, 0, 0))],
        out_specs=pl.BlockSpec((tq, nh * hd), lambda b, i: (b * nq + i, 0)),
        out_shape=jax.ShapeDtypeStruct((t, nh * hd), BF16),
        scratch_shapes=[pltpu.VMEM((tq, seq), jnp.int32),
                        pltpu.VMEM((seq, tq), jnp.int32),
                        pltpu.VMEM((tq, LANES), jnp.int32),
                        pltpu.VMEM((tq, ts), F32),
                        pltpu.VMEM((nh * tq, dl), BF16),
                        pltpu.VMEM((2, nh * tq, ts), F32),
                        pltpu.VMEM((2, nh * tq, LANES), F32),
                        pltpu.VMEM((nh * tq, LANES), F32),
                        pltpu.VMEM((nh * tq, LANES), F32),
                        pltpu.VMEM((nh * tq, LANES), F32),
                        pltpu.VMEM((nh * tq, dl), F32)],
        compiler_params=_params(2), name="dsa_attention")(zh, zs, zs, ckv, zh, w_uk, w_uv)


def _merge_kernel(ya_ref, yb_ref, yc_ref, wa_ref, wb_ref, wc_ref, g0_ref, g1_ref, g2_ref, b_ref, o_ref):
    b = b_ref[...]
    tm = o_ref.shape[0]
    sub = min(tm, MERGE_SUB_ROWS)
    for r0 in range(0, tm, sub):
        out = None
        for k, (y_ref, w_ref, g_ref) in enumerate(((ya_ref, wa_ref, g0_ref), (yb_ref, wb_ref, g1_ref),
                                                    (yc_ref, wc_ref, g2_ref))):
            gate = jax.nn.sigmoid(g_ref[r0:r0 + sub, :].astype(F32) + b[k:k + 1, :])
            term = gate * jnp.dot(y_ref[r0:r0 + sub, :], w_ref[...], preferred_element_type=F32)
            out = term if out is None else out + term
        o_ref[r0:r0 + sub, :] = out.astype(o_ref.dtype)


def _merge(ya, yb, yc, wa, wb, wc, layer, zg, b_gate, tm, tn):
    t = ya.shape[0]
    d = wa.shape[2]
    nj = d // tn

    def yspec(y):
        return pl.BlockSpec((tm, y.shape[1]), lambda i, j: (i, 0))

    def wspec(w):
        return pl.BlockSpec((None, w.shape[1], tn), lambda i, j: (layer, 0, j))

    def gspec(k):
        return pl.BlockSpec((tm, tn), lambda i, j: (i, k * nj + j))

    return pl.pallas_call(
        _merge_kernel, grid=(t // tm, nj),
        in_specs=[yspec(ya), yspec(yb), yspec(yc), wspec(wa), wspec(wb), wspec(wc),
                  gspec(0), gspec(1), gspec(2), pl.BlockSpec((b_gate.shape[0], tn), lambda i, j: (0, j))],
        out_specs=pl.BlockSpec((tm, tn), lambda i, j: (i, j)),
        out_shape=jax.ShapeDtypeStruct((t, d), BF16),
        compiler_params=_params(2), name="merge")(ya, yb, yc, wa, wb, wc, zg, zg, zg, b_gate)


def _ln_kernel(y_ref, g_ref, b_ref, o_ref, ob_ref):
    y = y_ref[...]
    mu = jnp.mean(y, axis=-1, keepdims=True)
    yc = y - mu
    var = jnp.mean(yc * yc, axis=-1, keepdims=True)
    out = yc * lax.rsqrt(var + LN_EPS) * g_ref[...] + b_ref[...]
    o_ref[...] = out
    ob_ref[...] = out.astype(BF16)


def _layer_norm(y, g, b, rows):
    t, d = y.shape
    row_spec = pl.BlockSpec((rows, d), lambda i: (i, 0))
    vec_spec = pl.BlockSpec((1, d), lambda i: (0, 0))
    return pl.pallas_call(
        _ln_kernel, grid=(t // rows,), in_specs=[row_spec, vec_spec, vec_spec],
        out_specs=[row_spec, row_spec],
        out_shape=[jax.ShapeDtypeStruct((t, d), F32), jax.ShapeDtypeStruct((t, d), BF16)],
        compiler_params=_params(1), name="layer_norm")(y, g, b)


def _ffn_up_kernel(x_ref, wg_ref, wu_ref, cw_ref, o_ref, buf_ref, carry_ref, *, tm, sub, tiles_per_seq):
    i = pl.program_id(0)
    j = pl.program_id(1)
    seq_start = i % tiles_per_seq == 0

    @pl.when(seq_start)
    def _():
        buf_ref[0:SUBLANES, :] = jnp.zeros((SUBLANES, buf_ref.shape[1]), F32)

    @pl.when(jnp.logical_not(seq_start))
    def _():
        buf_ref[0:SUBLANES, :] = carry_ref[j]

    w = cw_ref[...]
    for r0 in range(0, tm, sub):
        xs = x_ref[r0:r0 + sub, :]
        gt = jnp.dot(xs, wg_ref[...], preferred_element_type=F32)
        up = jnp.dot(xs, wu_ref[...], preferred_element_type=F32)
        buf_ref[SUBLANES + r0:SUBLANES + r0 + sub, :] = gt
        if r0 + sub == tm:
            carry_ref[j] = gt[sub - SUBLANES:sub, :]
        s1 = buf_ref[SUBLANES - 1 + r0:SUBLANES - 1 + r0 + sub, :]
        s2 = buf_ref[SUBLANES - 2 + r0:SUBLANES - 2 + r0 + sub, :]
        conv = w[0:1, :] * s2 + w[1:2, :] * s1 + w[2:3, :] * gt
        o_ref[r0:r0 + sub, :] = (jax.nn.silu(conv) * up).astype(o_ref.dtype)


def _ffn_up(xb, w_up, layer, conv_w, seq, tm, tn):
    t, d = xb.shape
    dff = conv_w.shape[1]
    assert seq % tm == 0 and dff % tn == 0 and w_up.shape[2] == 2 * dff
    nj = dff // tn
    return pl.pallas_call(
        functools.partial(_ffn_up_kernel, tm=tm, sub=min(tm, FFN_SUB_ROWS), tiles_per_seq=seq // tm),
        grid=(t // tm, nj),
        in_specs=[pl.BlockSpec((tm, d), lambda i, j: (i, 0)),
                  pl.BlockSpec((None, d, tn), lambda i, j: (layer, 0, j)),
                  pl.BlockSpec((None, d, tn), lambda i, j: (layer, 0, nj + j)),
                  pl.BlockSpec((CONV_W, tn), lambda i, j: (0, j))],
        out_specs=pl.BlockSpec((tm, tn), lambda i, j: (i, j)),
        out_shape=jax.ShapeDtypeStruct((t, dff), BF16),
        scratch_shapes=[pltpu.VMEM((SUBLANES + tm, tn), F32), pltpu.VMEM((nj, SUBLANES, tn), F32)],
        compiler_params=_params(2), name="ffn_up")(xb, w_up, w_up, conv_w)


def _tiles(seq, d_model):
    big = d_model >= 4096
    return dict(
        mm=(1024, 1024) if big else (256, 512),
        down=(512, 512) if big else (256, 512),
        ffn_up=(2048, 256) if big else (256, 256),
        merge=(1024, 512) if big else (256, 512),
        rows=512 if big else 256,
        ln_rows=512 if big else 256,
        tq=128, ts=512, slab=(32, 512))


def kernel(x, w_in, b_gate, conv_a, kv_norm, w_uk, w_uv, pool_w, pool_scale, w_br_a, w_br_b, w_br_c,
           w_o, ln1_g, ln1_b, w_up, conv_ffn_w, w_down, ln2_g, ln2_b):
    bsz, seq, d = x.shape
    depth = w_in.shape[0]
    alpha = (2.0 * depth) ** 0.25
    d_conv = conv_a.shape[2]
    n_heads, head_dim, d_lat = w_uk.shape[1:]
    d_att = n_heads * head_dim
    d_pool = pool_scale.shape[1]
    d_ff = conv_ffn_w.shape[2]
    n_in = w_in.shape[2]
    n_idx = (n_in - (3 * d_conv + d_att + d_lat + IDX_DIM + d_pool + b_gate.shape[1] * d)) // (IDX_DIM + 1)
    t = bsz * seq
    tl = _tiles(seq, d)

    o_bg, o_cg, o_v = 0, d_conv, 2 * d_conv
    o_q = 3 * d_conv
    o_ckv = o_q + d_att
    o_qi = o_ckv + d_lat
    o_ki = o_qi + n_idx * IDX_DIM
    o_wi = o_ki + IDX_DIM
    o_up = o_wi + n_idx
    o_g = o_up + d_pool
    assert o_g + b_gate.shape[1] * d == n_in
    assert IDX_DIM + n_idx <= LANES

    h_qi, h_q, h_a, h_up = 0, n_idx * IDX_DIM, n_idx * IDX_DIM + d_att, n_idx * IDX_DIM + d_att + 3 * d_conv
    n_h = h_up + d_pool
    assert h_q % d_att == 0 and h_a % d_conv == 0 and h_up % d_pool == 0 and h_qi % (n_idx * IDX_DIM) == 0
    n_s = d_lat + LANES

    wi = w_in.astype(BF16)
    w_h = jnp.concatenate([wi[:, :, o_qi:o_ki], wi[:, :, o_q:o_ckv], wi[:, :, o_bg:o_q], wi[:, :, o_up:o_g]], axis=2)
    w_s = jnp.concatenate([wi[:, :, o_ckv:o_qi], wi[:, :, o_ki:o_up],
                           jnp.zeros((depth, d, LANES - IDX_DIM - n_idx), BF16)], axis=2)
    w_g = wi[:, :, o_g:]
    wa_b, wb_b, wc_b = w_br_a.astype(BF16), w_br_b.astype(BF16), w_br_c.astype(BF16)
    wo_b, wup_b, wdn_b = w_o.astype(BF16), w_up.astype(BF16), w_down.astype(BF16)

    xf = x.reshape(t, d)
    xb = xf.astype(BF16)
    for l in range(depth):
        zh = _matmul(xb, w_h, l, BF16, *tl["mm"], name="proj_wide")
        zs = _matmul(xb, w_s, l, F32, tl["mm"][0], n_s, name="proj_small")
        zg = _matmul(xb, w_g, l, BF16, *tl["mm"], name="proj_gates")

        ya = _conv_a(zh, conv_a[l], bsz, seq, h_a // d_conv, d_conv, tl["rows"])
        yc = _pool(zh, pool_w[l].astype(BF16), pool_scale[l].reshape(1, d_pool), bsz, seq, h_up // d_pool,
                   tl["rows"])
        ckv = _ckv(zs, kv_norm[l].reshape(1, d_lat), d_lat, tl["rows"])
        yb = _attention(zh, zs, ckv, w_uk[l].astype(BF16), w_uv[l].astype(BF16), bsz, seq,
                        h_qi // (n_idx * IDX_DIM), h_q // d_att, n_idx, d_lat // LANES,
                        tl["tq"], min(tl["ts"], seq), *tl["slab"])

        merged = _merge(ya, yb, yc, wa_b, wb_b, wc_b, l, zg, b_gate[l], *tl["merge"])
        y1 = _matmul(merged, wo_b, l, F32, *tl["mm"], res=xf, alpha=alpha, name="proj_out")
        xf, xb = _layer_norm(y1, ln1_g[l].reshape(1, d), ln1_b[l].reshape(1, d), tl["ln_rows"])

        h = _ffn_up(xb, wup_b, l, conv_ffn_w[l], seq, *tl["ffn_up"])
        y2 = _matmul(h, wdn_b, l, F32, *tl["down"], res=xf, alpha=alpha, name="ffn_down")
        xf, xb = _layer_norm(y2, ln2_g[l].reshape(1, d), ln2_b[l].reshape(1, d), tl["ln_rows"])
    return xf.reshape(bsz, seq, d)
```
